```python
import math
import jax, jax.numpy as jnp
from jax import lax
import numpy as np

D_MODEL = 4096
BATCH = 4
SEQ = 2048
DEPTH = 2
DEC_BATCH = 128
DEC_SEQ = 4
PAST_LEN = 16384
PAGE_SIZE = 128

N_META = 16
D_A = D_MODEL
SC_WIDTH = 3
EXPAND = 2
D_INNER = EXPAND * D_MODEL
HEAD_DIM = 64
N_HEADS = D_INNER // HEAD_DIM
N_GROUPS = 8
HEADS_PER_GROUP = N_HEADS // N_GROUPS
D_STATE = 128
CONV_DIM = D_INNER + 2 * N_GROUPS * D_STATE
M_CONV = 4
CHUNK = 128
D_FF = ((8 * D_MODEL // 3 + 255) // 256) * 256
FF_CONV = 3
EPS = 1e-6
N_IN = 2 * D_MODEL + 3 * D_A + D_INNER + CONV_DIM + N_HEADS

kernel_name = "hybrid_shortconv_ssd_convffn_step"


def rmsnorm(x, g):
    xf = x.astype(jnp.float32)
    y = xf * lax.rsqrt(jnp.mean(xf * xf, axis=-1, keepdims=True) + EPS)
    return (y * g.astype(jnp.float32)).astype(x.dtype)


def causal_dwconv(x, buf, w, b=None):
    K = w.shape[0]
    L = x.shape[1]
    xp = jnp.concatenate([buf.astype(x.dtype), x], axis=1)
    y = sum(w[k] * xp[:, k:k + L] for k in range(K))
    if b is not None:
        y = y + b
    return y, xp[:, L:]


def masked_decay(cs):
    l = cs.shape[-1]
    mask = jnp.tril(jnp.ones((l, l), dtype=bool))
    diff = cs[..., :, None] - cs[..., None, :]
    return jnp.where(mask, jnp.exp(jnp.where(mask, diff, 0.0)), 0.0)


def ssd_scan(x, dt, A, B, C, h0, chunk):
    f32 = jnp.float32
    b, L = x.shape[:2]
    nc = -(-L // chunk)
    pad = nc * chunk - L
    x, dt, B, C = x.astype(f32), dt.astype(f32), B.astype(f32), C.astype(f32)
    if pad:
        padf = lambda t: jnp.pad(t, [(0, 0), (0, pad)] + [(0, 0)] * (t.ndim - 2))
        x, dt, B, C = padf(x), padf(dt), padf(B), padf(C)
    G, R = N_GROUPS, HEADS_PER_GROUP
    xs = (x * dt[..., None]).reshape(b, nc, chunk, G, R, HEAD_DIM)
    a = jnp.moveaxis((dt * A).reshape(b, nc, chunk, G, R), 2, -1)
    Bc = B.reshape(b, nc, chunk, G, D_STATE)
    Cc = C.reshape(b, nc, chunk, G, D_STATE)
    cs = jnp.cumsum(a, axis=-1)
    scores = jnp.einsum('bclgn,bcsgn->bcgls', Cc, Bc)
    y_diag = jnp.einsum('bcgls,bcgrls,bcsgrp->bclgrp', scores, masked_decay(cs), xs)
    decay_states = jnp.exp(cs[..., -1:] - cs)
    chunk_states = jnp.einsum('bclgn,bcgrl,bclgrp->bcgrpn', Bc, decay_states, xs)
    chunk_decay = jnp.exp(cs[..., -1])
    h0g = h0.astype(f32).reshape(b, G, R, HEAD_DIM, D_STATE)

    def step(h, inp):
        s, d = inp
        return h * d[..., None, None] + s, h

    h_final, h_in = lax.scan(step, h0g, (jnp.moveaxis(chunk_states, 1, 0), jnp.moveaxis(chunk_decay, 1, 0)))
    h_in = jnp.moveaxis(h_in, 0, 1)
    y_off = jnp.einsum('bclgn,bcgrpn,bcgrl->bclgrp', Cc, h_in, jnp.exp(cs))
    y = (y_diag + y_off).reshape(b, nc * chunk, N_HEADS, HEAD_DIM)[:, :L]
    return y, h_final.reshape(b, N_HEADS, HEAD_DIM, D_STATE)


def mamba_branch(z, xbc, dt_raw, conv_buf, h0, segments, conv_w, conv_b, dt_bias, a_log, d_skip, gnorm_w, wb_out):
    f32 = jnp.float32
    xbc, new_conv = causal_dwconv(xbc, conv_buf, conv_w, conv_b)
    xbc = jax.nn.silu(xbc)
    xh, Bm, Cm = jnp.split(xbc, [D_INNER, D_INNER + N_GROUPS * D_STATE], axis=-1)
    b, L = xh.shape[:2]
    xh = xh.reshape(b, L, N_HEADS, HEAD_DIM)
    Bm = Bm.reshape(b, L, N_GROUPS, D_STATE)
    Cm = Cm.reshape(b, L, N_GROUPS, D_STATE)
    dt = jax.nn.softplus(dt_raw.astype(f32) + dt_bias.astype(f32))
    A = -jnp.exp(a_log.astype(f32))
    h = h0
    ys = []
    start = 0
    for seg in segments:
        y_seg, h = ssd_scan(xh[:, start:start + seg], dt[:, start:start + seg], A,
                            Bm[:, start:start + seg], Cm[:, start:start + seg], h, min(CHUNK, seg))
        ys.append(y_seg)
        start += seg
    y = jnp.concatenate(ys, axis=1) if len(ys) > 1 else ys[0]
    y = y + d_skip.astype(f32)[:, None] * xh.astype(f32)
    y = y.reshape(b, L, D_INNER) * jax.nn.silu(z.astype(f32))
    yg = y.reshape(b, L, N_GROUPS, D_INNER // N_GROUPS)
    yg = yg * lax.rsqrt(jnp.mean(yg * yg, axis=-1, keepdims=True) + EPS)
    y = (yg.reshape(b, L, D_INNER) * gnorm_w.astype(f32)).astype(z.dtype)
    return y @ wb_out, new_conv, h.astype(z.dtype)


def layer(x, sc_buf, mc_buf, h0, fc_buf, segments, norm1_g, w_in, sconv_w, wa_out, mconv_w, mconv_b,
          dt_bias, a_log, d_skip, gnorm_w, wb_out, w_o, norm2_g, w_up, fconv_w, fconv_b, w_down):
    xn = rmsnorm(x, norm1_g)
    proj = xn @ w_in
    cuts = list(np.cumsum([D_MODEL, D_MODEL, D_A, D_A, D_A, D_INNER, CONV_DIM]))
    ga, gb, hA, bA, cA, z, xbc, dtr = jnp.split(proj, cuts, axis=-1)
    u, new_sc = causal_dwconv(cA * hA, sc_buf, sconv_w)
    out_a = (bA * u) @ wa_out
    out_b, new_mc, new_h = mamba_branch(z, xbc, dtr, mc_buf, h0, segments, mconv_w, mconv_b,
                                        dt_bias, a_log, d_skip, gnorm_w, wb_out)
    mix = jax.nn.sigmoid(ga) * out_a + jax.nn.sigmoid(gb) * out_b
    x = x + mix @ w_o
    xn2 = rmsnorm(x, norm2_g)
    up, new_fc = causal_dwconv(xn2 @ w_up, fc_buf, fconv_w, fconv_b)
    act, gate = jnp.split(up, 2, axis=-1)
    x = x + (jax.nn.silu(act) * gate) @ w_down
    return x, new_sc, new_mc, new_h, new_fc


def setup_inputs(seed: int = 0) -> dict:
    key = jax.random.key(seed)
    ks = jax.random.split(key, 32)
    nrm = lambda k, shape, s: jax.random.normal(k, shape, jnp.float32) * s
    dt0 = jnp.exp(jax.random.uniform(ks[14], (DEPTH, N_HEADS)) * (math.log(0.1) - math.log(0.001)) + math.log(0.001))
    return {
        "x_prompt": nrm(ks[0], (BATCH, SEQ, D_MODEL), 1.0),
        "x_sample": nrm(ks[1], (DEC_BATCH, DEC_SEQ, D_MODEL), 1.0),
        "state_sconv": nrm(ks[2], (DEPTH, DEC_BATCH, SC_WIDTH - 1, D_A), 1.0),
        "state_mconv": nrm(ks[3], (DEPTH, DEC_BATCH, M_CONV - 1, CONV_DIM), 1.0),
        "state_ssm": nrm(ks[4], (DEPTH, DEC_BATCH, N_HEADS, HEAD_DIM, D_STATE), 0.1),
        "state_fconv": nrm(ks[5], (DEPTH, DEC_BATCH, FF_CONV - 1, 2 * D_FF), 1.0),
        "meta_tokens": nrm(ks[6], (N_META, D_MODEL), 1.0),
        "norm1_g": 1.0 + nrm(ks[7], (DEPTH, D_MODEL), 0.02),
        "w_in": nrm(ks[8], (DEPTH, D_MODEL, N_IN), D_MODEL ** -0.5),
        "sconv_w": nrm(ks[9], (DEPTH, SC_WIDTH, D_A), SC_WIDTH ** -0.5),
        "wa_out": nrm(ks[10], (DEPTH, D_A, D_MODEL), D_A ** -0.5),
        "mconv_w": nrm(ks[11], (DEPTH, M_CONV, CONV_DIM), 0.5),
        "mconv_b": nrm(ks[12], (DEPTH, CONV_DIM), 0.01),
        "dt_bias": dt0 + jnp.log(-jnp.expm1(-dt0)),
        "a_log": jnp.log(jax.random.uniform(ks[15], (DEPTH, N_HEADS), jnp.float32, 1.0, 16.0)),
        "d_skip": 1.0 + nrm(ks[16], (DEPTH, N_HEADS), 0.01),
        "gnorm_w": 1.0 + nrm(ks[17], (DEPTH, D_INNER), 0.02),
        "wb_out": nrm(ks[18], (DEPTH, D_INNER, D_MODEL), D_INNER ** -0.5),
        "w_o": nrm(ks[19], (DEPTH, D_MODEL, D_MODEL), D_MODEL ** -0.5),
        "norm2_g": 1.0 + nrm(ks[20], (DEPTH, D_MODEL), 0.02),
        "w_up": nrm(ks[21], (DEPTH, D_MODEL, 2 * D_FF), D_MODEL ** -0.5),
        "fconv_w": nrm(ks[22], (DEPTH, FF_CONV, 2 * D_FF), FF_CONV ** -0.5),
        "fconv_b": nrm(ks[23], (DEPTH, 2 * D_FF), 0.01),
        "w_down": nrm(ks[24], (DEPTH, D_FF, D_MODEL), D_FF ** -0.5),
        "final_g": 1.0 + nrm(ks[25], (D_MODEL,), 0.02),
    }


def reference(x_prompt, x_sample, state_sconv, state_mconv, state_ssm, state_fconv, meta_tokens,
              norm1_g, w_in, sconv_w, wa_out, mconv_w, mconv_b, dt_bias, a_log, d_skip, gnorm_w,
              wb_out, w_o, norm2_g, w_up, fconv_w, fconv_b, w_down, final_g):
    dtype = x_prompt.dtype
    meta = jnp.broadcast_to(meta_tokens.astype(dtype)[None], (BATCH, N_META, D_MODEL))
    xp = jnp.concatenate([meta, x_prompt], axis=1)
    xs = x_sample
    p_segments = (N_META, SEQ)
    s_segments = (DEC_SEQ,)
    p_sc, p_mc, p_h, p_fc = [], [], [], []
    s_sc, s_mc, s_h, s_fc = [], [], [], []
    for l in range(DEPTH):
        params = (norm1_g[l], w_in[l], sconv_w[l], wa_out[l], mconv_w[l], mconv_b[l], dt_bias[l], a_log[l],
                  d_skip[l], gnorm_w[l], wb_out[l], w_o[l], norm2_g[l], w_up[l], fconv_w[l], fconv_b[l], w_down[l])
        xp, a1, a2, a3, a4 = layer(xp,
                                   jnp.zeros((BATCH, SC_WIDTH - 1, D_A), dtype),
                                   jnp.zeros((BATCH, M_CONV - 1, CONV_DIM), dtype),
                                   jnp.zeros((BATCH, N_HEADS, HEAD_DIM, D_STATE), jnp.float32),
                                   jnp.zeros((BATCH, FF_CONV - 1, 2 * D_FF), dtype),
                                   p_segments, *params)
        p_sc.append(a1); p_mc.append(a2); p_h.append(a3); p_fc.append(a4)
        xs, b1, b2, b3, b4 = layer(xs, state_sconv[l], state_mconv[l], state_ssm[l], state_fconv[l],
                                   s_segments, *params)
        s_sc.append(b1); s_mc.append(b2); s_h.append(b3); s_fc.append(b4)
    y_prompt = rmsnorm(xp, final_g)[:, N_META:]
    y_sample = rmsnorm(xs, final_g)
    return (y_prompt, y_sample,
            jnp.stack(p_sc), jnp.stack(p_mc), jnp.stack(p_h), jnp.stack(p_fc),
            jnp.stack(s_sc), jnp.stack(s_mc), jnp.stack(s_h), jnp.stack(s_fc))
```

```python
import functools
import math
from typing import NamedTuple

import jax
import jax.numpy as jnp
from jax import lax
from jax.experimental import pallas as pl
from jax.experimental.pallas import tpu as pltpu

F32 = jnp.float32
BF16 = jnp.bfloat16

N_META = 16
N_GROUPS = 8
HEAD_DIM = 64
SSD_CHUNK = 128
EPS = 1e-6

LANES = 128
SUBLANES = 8
BF16_ROWS = 16
VMEM_PHYSICAL_BYTES = 64 * 1024 * 1024
VMEM_BUDGET_BYTES = 56 * 1024 * 1024
COMPILER_SCRATCH_BYTES = 8 * 1024 * 1024
NORM_ROWS = 256


class Cfg(NamedTuple):
    d: int
    bp: int
    seq: int
    lp: int
    db: int
    ds: int
    h: int
    hpg: int
    gw: int
    n: int
    d_inner: int
    conv_dim: int
    d_ff: int
    dfp: int
    nt: int
    tm: int
    sb: int
    r: int
    nmain: int


def _largest_tile(dim, target, quantum=LANES):
    best = None
    t = quantum
    while t <= min(dim, target):
        if dim % t == 0:
            best = t
        t += quantum
    assert best is not None, (dim, target, quantum)
    return best


def _make_cfg(x_prompt, x_sample, state_ssm, w_up):
    bp, seq, d = x_prompt.shape
    db, ds, _ = x_sample.shape
    _, _, h, p, n = state_ssm.shape
    assert p == HEAD_DIM and h % N_GROUPS == 0 and seq % SSD_CHUNK == 0
    hpg = h // N_GROUPS
    assert hpg % 2 == 0 and hpg <= LANES
    d_inner = h * HEAD_DIM
    lp = N_META + seq
    d_ff = w_up.shape[-1] // 2
    nt = 3 if (lp % (3 * BF16_ROWS) == 0 and lp // 3 >= db * ds) else 1
    tm = lp // nt
    assert tm % BF16_ROWS == 0 and db * ds <= tm and ds >= 3
    sb = bp * nt
    ktile = 512
    dfp = -(-d_ff // ktile) * ktile
    return Cfg(d=d, bp=bp, seq=seq, lp=lp, db=db, ds=ds, h=h, hpg=hpg, gw=hpg * HEAD_DIM, n=n,
               d_inner=d_inner, conv_dim=d_inner + 2 * N_GROUPS * n, d_ff=d_ff, dfp=dfp, nt=nt, tm=tm,
               sb=sb, r=(sb + 1) * tm, nmain=5 * d + d_inner + d_inner + 2 * N_GROUPS * n)


def _cparams(semantics, block_bytes, temp_bytes=0):
    need = 2 * block_bytes + temp_bytes + COMPILER_SCRATCH_BYTES
    return pltpu.CompilerParams(dimension_semantics=semantics,
                                vmem_limit_bytes=int(min(max(need, 16 << 20), VMEM_BUDGET_BYTES)))


def _nbytes(shape, dtype):
    return math.prod(shape) * jnp.dtype(dtype).itemsize


def _silu(x):
    return x * jax.nn.sigmoid(x)


def _softplus(x):
    return jnp.maximum(x, 0.0) + jnp.log1p(jnp.exp(-jnp.abs(x)))


def _shift_down(x, s):
    return x if s == 0 else pltpu.roll(x, s, axis=0)


def _shift_up(x, s):
    return x if s == 0 else pltpu.roll(x, x.shape[0] - s, axis=0)


def _rows(shape):
    return lax.broadcasted_iota(jnp.int32, shape, 0)


def _dot(a, b):
    return jnp.dot(a, b, preferred_element_type=F32)


def _dot_nt(a, b):
    return lax.dot_general(a, b, (((1,), (1,)), ((), ())), preferred_element_type=F32)


def _expand_heads(v, e):
    hi = v.astype(BF16)
    lo = (v - hi.astype(F32)).astype(BF16)
    return _dot(hi, e) + _dot(lo, e)


def _rmsnorm_kernel(x_ref, g_ref, o_ref):
    x = x_ref[...]
    y = x * lax.rsqrt(jnp.mean(x * x, axis=-1, keepdims=True) + EPS)
    o_ref[...] = (y * g_ref[...]).astype(o_ref.dtype)


def _rmsnorm(cfg, x, g, out_dtype):
    tr = _largest_tile(cfg.r, NORM_ROWS, BF16_ROWS)
    blk = _nbytes((tr, cfg.d), F32) + _nbytes((tr, cfg.d), out_dtype)
    return pl.pallas_call(
        _rmsnorm_kernel,
        grid=(cfg.r // tr,),
        in_specs=[pl.BlockSpec((tr, cfg.d), lambda i: (i, 0)),
                  pl.BlockSpec((1, cfg.d), lambda i: (0, 0))],
        out_specs=pl.BlockSpec((tr, cfg.d), lambda i: (i, 0)),
        out_shape=jax.ShapeDtypeStruct((cfg.r, cfg.d), out_dtype),
        compiler_params=_cparams(("parallel",), blk, _nbytes((tr, cfg.d), F32)),
        name="rmsnorm",
    )(x, g.reshape(1, cfg.d))


def _add_rmsnorm_kernel(x_ref, d_ref, g_ref, xo_ref, no_ref):
    x = x_ref[...] + d_ref[...]
    xo_ref[...] = x
    y = x * lax.rsqrt(jnp.mean(x * x, axis=-1, keepdims=True) + EPS)
    no_ref[...] = (y * g_ref[...]).astype(no_ref.dtype)


def _add_rmsnorm(cfg, x, delta, g, out_dtype):
    tr = _largest_tile(cfg.r, NORM_ROWS, BF16_ROWS)
    blk = 3 * _nbytes((tr, cfg.d), F32) + _nbytes((tr, cfg.d), out_dtype)
    row = pl.BlockSpec((tr, cfg.d), lambda i: (i, 0))
    return pl.pallas_call(
        _add_rmsnorm_kernel,
        grid=(cfg.r // tr,),
        in_specs=[row, row, pl.BlockSpec((1, cfg.d), lambda i: (0, 0))],
        out_specs=[row, row],
        out_shape=[jax.ShapeDtypeStruct((cfg.r, cfg.d), F32),
                   jax.ShapeDtypeStruct((cfg.r, cfg.d), out_dtype)],
        compiler_params=_cparams(("parallel",), blk, _nbytes((tr, cfg.d), F32)),
        name="add_rmsnorm",
    )(x, delta, g.reshape(1, cfg.d))


def _mm_fullk_kernel(x_ref, w_ref, o_ref):
    o_ref[...] = _dot(x_ref[...], w_ref[...]).astype(o_ref.dtype)


def _mm_fullk(cfg, x, w, ncols, col0, out_dtype, tn_target=1024, name="mm_fullk"):
    k = x.shape[1]
    tn = _largest_tile(math.gcd(ncols, col0) if col0 else ncols, tn_target)
    cb0 = col0 // tn
    blk = _nbytes((cfg.tm, k), BF16) + _nbytes((k, tn), BF16) + _nbytes((cfg.tm, tn), out_dtype)
    return pl.pallas_call(
        _mm_fullk_kernel,
        grid=(cfg.r // cfg.tm, ncols // tn),
        in_specs=[pl.BlockSpec((cfg.tm, k), lambda i, j: (i, 0)),
                  pl.BlockSpec((k, tn), lambda i, j: (0, j + cb0))],
        out_specs=pl.BlockSpec((cfg.tm, tn), lambda i, j: (i, j)),
        out_shape=jax.ShapeDtypeStruct((cfg.r, ncols), out_dtype),
        compiler_params=_cparams(("parallel", "arbitrary"), blk, _nbytes((cfg.tm, tn), F32)),
        name=name,
    )(x, w)


def _mm_fulln_kernel(x_ref, w_ref, o_ref):
    k = pl.program_id(1)
    part = _dot(x_ref[...], w_ref[...])

    @pl.when(k == 0)
    def _():
        o_ref[...] = part

    @pl.when(k > 0)
    def _():
        o_ref[...] += part


def _mm_fulln(cfg, x, w, tk_target=512, name="mm_fulln"):
    k, n = w.shape
    tk = _largest_tile(k, tk_target)
    blk = _nbytes((cfg.tm, tk), BF16) + _nbytes((tk, n), BF16) + _nbytes((cfg.tm, n), F32)
    return pl.pallas_call(
        _mm_fulln_kernel,
        grid=(cfg.r // cfg.tm, k // tk),
        in_specs=[pl.BlockSpec((cfg.tm, tk), lambda i, kk: (i, kk)),
                  pl.BlockSpec((tk, n), lambda i, kk: (kk, 0))],
        out_specs=pl.BlockSpec((cfg.tm, n), lambda i, kk: (i, 0)),
        out_shape=jax.ShapeDtypeStruct((cfg.r, n), F32),
        compiler_params=_cparams(("parallel", "arbitrary"), blk, _nbytes((cfg.tm, n), F32)),
        name=name,
    )(x, w)


def _prompt_taps(x, k):
    row = _rows(x.shape)
    return [x] + [jnp.where(row >= s, _shift_down(x, s), 0.0) for s in range(1, k)]


def _sample_taps(x, e, k, ds):
    t = _rows(x.shape) % ds
    return [x] + [jnp.where(t >= s, _shift_down(x, s), _shift_up(e, k - 1 - s)) for s in range(1, k)]


def _conv_from_taps(taps, w, b=None):
    k = len(taps)
    y = taps[0] * w[k - 1:k]
    for s in range(1, k):
        y = y + taps[s] * w[k - 1 - s:k - s]
    return y if b is None else y + b


def _sconv_prompt_kernel(h_ref, b_ref, c_ref, w_ref, v_ref, st_ref, *, lp):
    ch = c_ref[...].astype(F32) * h_ref[...].astype(F32)
    u = _conv_from_taps(_prompt_taps(ch, 3), w_ref[...])
    v_ref[...] = (b_ref[...].astype(F32) * u).astype(v_ref.dtype)
    st_ref[0] = ch[lp - SUBLANES:lp][SUBLANES - 2:]


def _sconv_sample_kernel(h_ref, b_ref, c_ref, e_ref, w_ref, v_ref, ch_ref, *, ds):
    ch = c_ref[...].astype(F32) * h_ref[...].astype(F32)
    u = _conv_from_taps(_sample_taps(ch, e_ref[...], 3, ds), w_ref[...])
    v_ref[...] = (b_ref[...].astype(F32) * u).astype(v_ref.dtype)
    ch_ref[...] = ch


def _sconv(cfg, proj, e_sc, w):
    d = cfg.d
    tc = _largest_tile(d, 512)
    nb = d // tc
    pblk = lambda off: pl.BlockSpec((cfg.lp, tc), lambda b, j: (b, j + off * nb))
    blk = 4 * _nbytes((cfg.lp, tc), BF16)
    v_p, st_p = pl.pallas_call(
        functools.partial(_sconv_prompt_kernel, lp=cfg.lp),
        grid=(cfg.bp, nb),
        in_specs=[pblk(2), pblk(3), pblk(4), pl.BlockSpec((3, tc), lambda b, j: (0, j))],
        out_specs=[pl.BlockSpec((cfg.lp, tc), lambda b, j: (b, j)),
                   pl.BlockSpec((1, 2, tc), lambda b, j: (b, 0, j))],
        out_shape=[jax.ShapeDtypeStruct((cfg.r, d), BF16),
                   jax.ShapeDtypeStruct((cfg.bp, 2, d), F32)],
        compiler_params=_cparams(("parallel", "parallel"), blk, 6 * _nbytes((cfg.lp, tc), F32)),
        name="sconv_prompt",
    )(proj, proj, proj, w)
    sblk = lambda off: pl.BlockSpec((cfg.tm, tc), lambda j: (cfg.sb, j + off * nb))
    own = pl.BlockSpec((cfg.tm, tc), lambda j: (0, j))
    blk = 4 * _nbytes((cfg.tm, tc), BF16) + 2 * _nbytes((cfg.tm, tc), F32)
    v_s, ch_s = pl.pallas_call(
        functools.partial(_sconv_sample_kernel, ds=cfg.ds),
        grid=(nb,),
        in_specs=[sblk(2), sblk(3), sblk(4), own, pl.BlockSpec((3, tc), lambda j: (0, j))],
        out_specs=[own, own],
        out_shape=[jax.ShapeDtypeStruct((cfg.tm, d), BF16),
                   jax.ShapeDtypeStruct((cfg.tm, d), F32)],
        compiler_params=_cparams(("parallel",), blk, 6 * _nbytes((cfg.tm, tc), F32)),
        name="sconv_sample",
    )(proj, proj, proj, e_sc, w)
    v = lax.dynamic_update_slice(v_p, v_s, (cfg.sb * cfg.tm, 0))
    st_s = ch_s[:cfg.db * cfg.ds].reshape(cfg.db, cfg.ds, d)[:, cfg.ds - 2:]
    return v, st_p, st_s


def _ffn_act(taps_a, taps_g, w_ref, b_ref, dfp_tile):
    del dfp_tile
    wa, wg = w_ref[0], w_ref[1]
    ba, bg = b_ref[0], b_ref[1]
    act = _conv_from_taps(taps_a, wa, ba)
    gate = _conv_from_taps(taps_g, wg, bg)
    return _silu(act) * gate


def _ffn_prompt_kernel(a_ref, g_ref, w_ref, b_ref, h_ref, st_ref, *, lp):
    a = a_ref[...].astype(F32)
    g = g_ref[...].astype(F32)
    h_ref[...] = _ffn_act(_prompt_taps(a, 3), _prompt_taps(g, 3), w_ref, b_ref, None).astype(h_ref.dtype)
    st_ref[0, 0] = a[lp - SUBLANES:lp][SUBLANES - 2:]
    st_ref[0, 1] = g[lp - SUBLANES:lp][SUBLANES - 2:]


def _ffn_sample_kernel(a_ref, g_ref, ea_ref, eg_ref, w_ref, b_ref, h_ref, *, ds):
    a = a_ref[...].astype(F32)
    g = g_ref[...].astype(F32)
    h_ref[...] = _ffn_act(_sample_taps(a, ea_ref[...], 3, ds), _sample_taps(g, eg_ref[...], 3, ds),
                          w_ref, b_ref, None).astype(h_ref.dtype)


def _ffn_gate(cfg, up, e_fc, w2, b2):
    dfp = cfg.dfp
    tc = _largest_tile(dfp, 512)
    nb = dfp // tc
    wspec = lambda f: pl.BlockSpec((2, 3, tc), f)
    bspec = lambda f: pl.BlockSpec((2, 1, tc), f)
    blk = 3 * _nbytes((cfg.lp, tc), BF16)
    h_p, st_p = pl.pallas_call(
        functools.partial(_ffn_prompt_kernel, lp=cfg.lp),
        grid=(cfg.bp, nb),
        in_specs=[pl.BlockSpec((cfg.lp, tc), lambda b, j: (b, j)),
                  pl.BlockSpec((cfg.lp, tc), lambda b, j: (b, j + nb)),
                  wspec(lambda b, j: (0, 0, j)), bspec(lambda b, j: (0, 0, j))],
        out_specs=[pl.BlockSpec((cfg.lp, tc), lambda b, j: (b, j)),
                   pl.BlockSpec((1, 2, 2, tc), lambda b, j: (b, 0, 0, j))],
        out_shape=[jax.ShapeDtypeStruct((cfg.r, dfp), BF16),
                   jax.ShapeDtypeStruct((cfg.bp, 2, 2, dfp), F32)],
        compiler_params=_cparams(("parallel", "parallel"), blk, 10 * _nbytes((cfg.lp, tc), F32)),
        name="ffn_gate_prompt",
    )(up, up, w2, b2)
    blk = 3 * _nbytes((cfg.tm, tc), BF16) + 2 * _nbytes((cfg.tm, tc), F32)
    h_s = pl.pallas_call(
        functools.partial(_ffn_sample_kernel, ds=cfg.ds),
        grid=(nb,),
        in_specs=[pl.BlockSpec((cfg.tm, tc), lambda j: (cfg.sb, j)),
                  pl.BlockSpec((cfg.tm, tc), lambda j: (cfg.sb, j + nb)),
                  pl.BlockSpec((cfg.tm, tc), lambda j: (0, j)),
                  pl.BlockSpec((cfg.tm, tc), lambda j: (0, j + nb)),
                  wspec(lambda j: (0, 0, j)), bspec(lambda j: (0, 0, j))],
        out_specs=pl.BlockSpec((cfg.tm, tc), lambda j: (0, j)),
        out_shape=jax.ShapeDtypeStruct((cfg.tm, dfp), BF16),
        compiler_params=_cparams(("parallel",), blk, 10 * _nbytes((cfg.tm, tc), F32)),
        name="ffn_gate_sample",
    )(up, up, e_fc, e_fc, w2, b2)
    h = lax.dynamic_update_slice(h_p, h_s, (cfg.sb * cfg.tm, 0))
    return h, st_p


def _mix_kernel(ga_ref, gb_ref, a_ref, b_ref, o_ref):
    ga = jax.nn.sigmoid(ga_ref[...].astype(F32))
    gb = jax.nn.sigmoid(gb_ref[...].astype(F32))
    o_ref[...] = (ga * a_ref[...] + gb * b_ref[...]).astype(o_ref.dtype)


def _mix(cfg, proj, out_a, out_b):
    d = cfg.d
    tc = _largest_tile(d, 1024)
    nb = d // tc
    blk = 3 * _nbytes((cfg.tm, tc), BF16) + 2 * _nbytes((cfg.tm, tc), F32)
    own = pl.BlockSpec((cfg.tm, tc), lambda i, j: (i, j))
    return pl.pallas_call(
        _mix_kernel,
        grid=(cfg.r // cfg.tm, nb),
        in_specs=[own, pl.BlockSpec((cfg.tm, tc), lambda i, j: (i, j + nb)), own, own],
        out_specs=own,
        out_shape=jax.ShapeDtypeStruct((cfg.r, d), BF16),
        compiler_params=_cparams(("parallel", "parallel"), blk, 4 * _nbytes((cfg.tm, tc), F32)),
        name="mix",
    )(proj, proj, out_a, out_b)


def _ssd_gate_norm(y, z, gnw):
    y = y * _silu(z)
    return y * lax.rsqrt(jnp.mean(y * y, axis=-1, keepdims=True) + EPS) * gnw


def _ssd_prompt_kernel(x_ref, bm_ref, cm_ref, z_ref, dtr_ref, wx_ref, wb_ref, wc_ref, bx_ref, bb_ref, bc_ref,
                       dtb_ref, alog_ref, dsk_ref, gnw_ref, e_ref, tril_ref,
                       y_ref, hfin_ref, mcx_ref, mcb_ref, mcc_ref,
                       ht_ref, hx_ref, hb_ref, hc_ref, *, cfg):
    lp, hpg, n, gw = cfg.lp, cfg.hpg, cfg.n, cfg.gw
    c = SSD_CHUNK
    nch = 1 + cfg.seq // c
    ht_ref[...] = jnp.zeros_like(ht_ref)
    hx_ref[...] = jnp.zeros_like(hx_ref)
    hb_ref[...] = jnp.zeros_like(hb_ref)
    hc_ref[...] = jnp.zeros_like(hc_ref)
    a_neg = -jnp.exp(alog_ref[0])
    dtb = dtb_ref[0]
    e = e_ref[...]
    tril = tril_ref[...]
    li = lax.broadcasted_iota(jnp.int32, (c, c), 0)
    si = lax.broadcasted_iota(jnp.int32, (c, c), 1)
    causal = li >= si
    lane_lo = lax.broadcasted_iota(jnp.int32, (c, 2 * HEAD_DIM), 1) < HEAD_DIM

    def conv_silu(cur, halo_ref, w_ref, b_ref, first):
        ext = jnp.concatenate([halo_ref[...], cur], axis=0)
        taps = [cur] + [_shift_down(ext, s)[SUBLANES:] for s in range(1, 4)]
        halo_ref[...] = jnp.where(first, cur[N_META - SUBLANES:N_META], cur[c - SUBLANES:])
        return _silu(_conv_from_taps(taps, w_ref[...], b_ref[...]))

    def chunk(k, carry):
        first = k == 0
        start = pl.multiple_of(jnp.where(first, 0, N_META + c * (k - 1)), BF16_ROWS)
        rows = pl.ds(start, c)
        xh = conv_silu(x_ref[rows, :].astype(F32), hx_ref, wx_ref, bx_ref, first)
        bc = conv_silu(bm_ref[rows, :].astype(F32), hb_ref, wb_ref, bb_ref, first)
        cc = conv_silu(cm_ref[rows, :].astype(F32), hc_ref, wc_ref, bc_ref, first)
        live = jnp.logical_or(jnp.logical_not(first), _rows((c, LANES)) < N_META)
        dt = jnp.where(live, _softplus(dtr_ref[rows, :] + dtb), 0.0)
        a = dt * a_neg
        a1 = a.astype(BF16)
        r1 = a - a1.astype(F32)
        a2 = r1.astype(BF16)
        a3 = (r1 - a2.astype(F32)).astype(BF16)
        cs = _dot(tril, a1) + _dot(tril, a2) + _dot(tril, a3)
        cs_t = cs.T
        cs_last = cs[c - 1:c]
        dt_e = _expand_heads(dt, e)
        w_e = _expand_heads(jnp.exp(cs_last - cs), e)
        ecs_e = _expand_heads(jnp.exp(cs), e)
        xs = xh * dt_e
        xs_b = xs.astype(BF16)
        scores = _dot_nt(cc.astype(BF16), bc.astype(BF16))
        pairs = []
        for q in range(hpg // 2):
            res = []
            for r in (2 * q, 2 * q + 1):
                diff = jnp.where(causal, cs[:, r:r + 1] - cs_t[r:r + 1, :], 0.0)
                m = jnp.where(causal, scores * jnp.exp(diff), 0.0).astype(BF16)
                res.append(_dot(m, xs_b[:, 2 * HEAD_DIM * q:2 * HEAD_DIM * (q + 1)]))
            pairs.append(jnp.where(lane_lo, res[0], res[1]))
        y = jnp.concatenate(pairs, axis=1) if len(pairs) > 1 else pairs[0]
        ht = ht_ref[...]
        y = y + _dot(cc.astype(BF16), ht.astype(BF16)) * ecs_e
        ht_ref[...] = ht * ecs_e[c - 1:c] + _dot(bc.T.astype(BF16), (xs * w_e).astype(BF16))
        y = y + dsk_ref[...] * xh
        y_ref[rows, :] = _ssd_gate_norm(y, z_ref[rows, :].astype(F32), gnw_ref[...]).astype(y_ref.dtype)
        return carry

    lax.fori_loop(0, nch, chunk, 0)
    hfin_ref[0] = ht_ref[...].T.reshape(hpg, HEAD_DIM, n)
    tail = lambda ref: ref[lp - BF16_ROWS:lp, :].astype(F32)[BF16_ROWS - 3:]
    mcx_ref[0] = tail(x_ref)
    mcb_ref[0] = tail(bm_ref)
    mcc_ref[0] = tail(cm_ref)


def _ssd_prompt(cfg, proj, dtr_g, prm):
    lp, gw, n, hpg, g_ = cfg.lp, cfg.gw, cfg.n, cfg.hpg, N_GROUPS
    xb0 = 7 * cfg.d // gw
    zb0 = 5 * cfg.d // gw
    bb0 = 9 * cfg.d // n
    cvb = cfg.d_inner // n
    colblk = lambda w, f: pl.BlockSpec((lp, w), f)
    par = lambda rows, w, f: pl.BlockSpec((rows, w), f)
    vec3 = pl.BlockSpec((1, 1, LANES), lambda b, g: (g, 0, 0))
    in_specs = [
        colblk(gw, lambda b, g: (b, xb0 + g)), colblk(n, lambda b, g: (b, bb0 + g)),
        colblk(n, lambda b, g: (b, bb0 + g_ + g)), colblk(gw, lambda b, g: (b, zb0 + g)),
        colblk(LANES, lambda b, g: (b, g)),
        par(4, gw, lambda b, g: (0, g)), par(4, n, lambda b, g: (0, cvb + g)), par(4, n, lambda b, g: (0, cvb + g_ + g)),
        par(1, gw, lambda b, g: (0, g)), par(1, n, lambda b, g: (0, cvb + g)), par(1, n, lambda b, g: (0, cvb + g_ + g)),
        vec3, vec3,
        par(1, gw, lambda b, g: (0, g)), par(1, gw, lambda b, g: (0, g)),
        par(LANES, gw, lambda b, g: (0, 0)), par(SSD_CHUNK, SSD_CHUNK, lambda b, g: (0, 0)),
    ]
    out_specs = [
        pl.BlockSpec((lp, gw), lambda b, g: (b, g)),
        pl.BlockSpec((1, hpg, HEAD_DIM, n), lambda b, g: (b, g, 0, 0)),
        pl.BlockSpec((1, 3, gw), lambda b, g: (b, 0, g)),
        pl.BlockSpec((1, 3, n), lambda b, g: (b, 0, g)),
        pl.BlockSpec((1, 3, n), lambda b, g: (b, 0, g)),
    ]
    out_shape = [
        jax.ShapeDtypeStruct((cfg.r, cfg.d_inner), BF16),
        jax.ShapeDtypeStruct((cfg.bp, cfg.h, HEAD_DIM, n), F32),
        jax.ShapeDtypeStruct((cfg.bp, 3, cfg.d_inner), F32),
        jax.ShapeDtypeStruct((cfg.bp, 3, g_ * n), F32),
        jax.ShapeDtypeStruct((cfg.bp, 3, g_ * n), F32),
    ]
    blk = 3 * _nbytes((lp, gw), BF16) + 2 * _nbytes((lp, n), BF16) + _nbytes((lp, LANES), F32)
    y, hfin, mcx, mcb, mcc = pl.pallas_call(
        functools.partial(_ssd_prompt_kernel, cfg=cfg),
        grid=(cfg.bp, g_),
        in_specs=in_specs, out_specs=out_specs, out_shape=out_shape,
        scratch_shapes=[pltpu.VMEM((n, gw), F32), pltpu.VMEM((SUBLANES, gw), F32),
                        pltpu.VMEM((SUBLANES, n), F32), pltpu.VMEM((SUBLANES, n), F32)],
        compiler_params=_cparams(("parallel", "parallel"), blk, 24 * _nbytes((SSD_CHUNK, gw), F32) + (4 << 20)),
        name="ssd_prompt",
    )(proj, proj, proj, proj, dtr_g,
      prm["mconv_w"], prm["mconv_w"], prm["mconv_w"], prm["mconv_b"], prm["mconv_b"], prm["mconv_b"],
      prm["dt_bias_g"], prm["a_log_g"], prm["d_skip_e"], prm["gnorm_w"], prm["expand"], prm["tril"])
    return y, hfin, jnp.concatenate([mcx, mcb, mcc], axis=-1)


def _ssd_tok_kernel(x_ref, bm_ref, cm_ref, dtr_ref, ex_ref, eb_ref, ec_ref, wx_ref, wb_ref, wc_ref,
                    bx_ref, bb_ref, bc_ref, dtb_ref, alog_ref, dsk_ref, e_ref,
                    ypre_ref, ecs_ref, xsw_ref, bo_ref, co_ref, dec_ref, *, ds):
    e = e_ref[...]
    conv = lambda ref, eref, w, b: _silu(_conv_from_taps(
        _sample_taps(ref[...].astype(F32), eref[...], 4, ds), w[...], b[...]))
    xh = conv(x_ref, ex_ref, wx_ref, bx_ref)
    bc = conv(bm_ref, eb_ref, wb_ref, bb_ref)
    cc = conv(cm_ref, ec_ref, wc_ref, bc_ref)
    dt = _softplus(dtr_ref[...] + dtb_ref[0])
    a = dt * (-jnp.exp(alog_ref[0]))
    t = _rows(a.shape) % ds
    t1 = _rows((a.shape[0], 1)) % ds
    cs = a
    for j in range(1, ds):
        cs = cs + jnp.where(t >= j, _shift_down(a, j), 0.0)
    cs_last = jnp.zeros_like(cs)
    for j in range(ds):
        cs_last = cs_last + jnp.where(t == ds - 1 - j, _shift_up(cs, j), 0.0)
    ecs_e = _expand_heads(jnp.exp(cs), e)
    xs = xh * _expand_heads(dt, e)
    y = dsk_ref[...] * xh
    for j in range(ds):
        sc = jnp.sum(cc * _shift_down(bc, j), axis=-1, keepdims=True)
        diff = jnp.where(t >= j, cs - _shift_down(cs, j), 0.0)
        coef = jnp.where(t1 >= j, sc, 0.0) * jnp.exp(diff)
        y = y + _expand_heads(coef, e) * _shift_down(xs, j)
    ypre_ref[...] = y
    ecs_ref[...] = ecs_e
    xsw_ref[...] = xs * _expand_heads(jnp.exp(cs_last - cs), e)
    bo_ref[...] = bc
    co_ref[...] = cc
    dec_ref[...] = jnp.exp(cs_last)


def _ssd_state_kernel(dec_ref, h0_ref, c_ref, b_ref, xsw_ref, yinit_ref, hout_ref, yoff_ref, *, cfg):
    del yinit_ref
    hpg, n, gw, ds = cfg.hpg, cfg.n, cfg.gw, cfg.ds
    i = pl.program_id(0)
    nseq = SUBLANES // ds
    pad = jnp.zeros((LANES - SUBLANES, n), F32)
    row8 = _rows((SUBLANES, 1)) // ds
    cmat = c_ref[...]
    bmat = b_ref[...]
    xsw = xsw_ref[...]
    youts = []
    for g in range(N_GROUPS):
        cg = cmat[:, g * n:(g + 1) * n].astype(BF16)
        bg = bmat[:, g * n:(g + 1) * n]
        xw = jnp.concatenate([xsw[:, g * gw:(g + 1) * gw], jnp.zeros((LANES - SUBLANES, gw), F32)], axis=0)
        xw_t = xw.T.astype(BF16)
        yg = jnp.zeros((SUBLANES, gw), F32)
        for s in range(nseq):
            h0 = h0_ref[s, g * hpg:(g + 1) * hpg].reshape(gw, n)
            yg = jnp.where(row8 == s, _dot_nt(cg, h0.astype(BF16)), yg)
            bs = jnp.concatenate([jnp.where(row8 == s, bg, 0.0), pad], axis=0).astype(BF16)
            dh = _dot(xw_t, bs)
            for r in range(hpg):
                sl = slice(r * HEAD_DIM, (r + 1) * HEAD_DIM)
                hout_ref[s, g * hpg + r] = h0[sl] * dec_ref[i * nseq + s, g * hpg + r] + dh[sl]
        youts.append(yg)
    yoff_ref[...] = jnp.concatenate(youts, axis=1)


def _ssd_post_kernel(ypre_ref, yoff_ref, ecs_ref, z_ref, gnw_ref, o_ref):
    y = ypre_ref[...] + yoff_ref[...] * ecs_ref[...]
    o_ref[...] = _ssd_gate_norm(y, z_ref[...].astype(F32), gnw_ref[...]).astype(o_ref.dtype)


def _ssd_sample(cfg, proj, dtr_g, e_mc, h0, prm):
    tm, gw, n, g_, sb = cfg.tm, cfg.gw, cfg.n, N_GROUPS, cfg.sb
    xb0 = 7 * cfg.d // gw
    zb0 = 5 * cfg.d // gw
    bb0 = 9 * cfg.d // n
    cvb = cfg.d_inner // n
    row = lambda w, f: pl.BlockSpec((tm, w), f)
    par = lambda rows, w, f: pl.BlockSpec((rows, w), f)
    vec3 = pl.BlockSpec((1, 1, LANES), lambda g: (g, 0, 0))
    in_specs = [
        row(gw, lambda g: (sb, xb0 + g)), row(n, lambda g: (sb, bb0 + g)), row(n, lambda g: (sb, bb0 + g_ + g)),
        row(LANES, lambda g: (sb, g)),
        row(gw, lambda g: (0, g)), row(n, lambda g: (0, cvb + g)), row(n, lambda g: (0, cvb + g_ + g)),
        par(4, gw, lambda g: (0, g)), par(4, n, lambda g: (0, cvb + g)), par(4, n, lambda g: (0, cvb + g_ + g)),
        par(1, gw, lambda g: (0, g)), par(1, n, lambda g: (0, cvb + g)), par(1, n, lambda g: (0, cvb + g_ + g)),
        vec3, vec3, par(1, gw, lambda g: (0, g)), par(LANES, gw, lambda g: (0, 0)),
    ]
    own = lambda w: pl.BlockSpec((tm, w), lambda g: (0, g))
    blk = 2 * _nbytes((tm, gw), BF16) + 5 * _nbytes((tm, gw), F32)
    ypre, ecs, xsw, bo, co, dec = pl.pallas_call(
        functools.partial(_ssd_tok_kernel, ds=cfg.ds),
        grid=(g_,),
        in_specs=in_specs,
        out_specs=[own(gw), own(gw), own(gw), own(n), own(n), own(LANES)],
        out_shape=[jax.ShapeDtypeStruct((tm, cfg.d_inner), F32)] * 3
                  + [jax.ShapeDtypeStruct((tm, g_ * n), F32)] * 2
                  + [jax.ShapeDtypeStruct((tm, g_ * LANES), F32)],
        compiler_params=_cparams(("parallel",), blk, 16 * _nbytes((tm, gw), F32)),
        name="ssd_tok",
    )(proj, proj, proj, dtr_g, e_mc, e_mc, e_mc,
      prm["mconv_w"], prm["mconv_w"], prm["mconv_w"], prm["mconv_b"], prm["mconv_b"], prm["mconv_b"],
      prm["dt_bias_g"], prm["a_log_g"], prm["d_skip_e"], prm["expand"])
    nrow = cfg.db * cfg.ds
    dec_sh = dec[:nrow].reshape(cfg.db, cfg.ds, g_, LANES)[:, cfg.ds - 1, :, :cfg.hpg].reshape(cfg.db, cfg.h)
    nseq = SUBLANES // cfg.ds
    rows8 = lambda w: pl.BlockSpec((SUBLANES, w), lambda i: (i, 0))
    hblk = pl.BlockSpec((nseq, cfg.h, HEAD_DIM, n), lambda i: (i, 0, 0, 0))
    blk = 2 * _nbytes((nseq, cfg.h, HEAD_DIM, n), F32) + 3 * _nbytes((SUBLANES, cfg.d_inner), F32)
    hout, yoff = pl.pallas_call(
        functools.partial(_ssd_state_kernel, cfg=cfg),
        grid=(cfg.db // nseq,),
        in_specs=[pl.BlockSpec(memory_space=pltpu.SMEM), hblk, rows8(g_ * n), rows8(g_ * n), rows8(cfg.d_inner),
                  pl.BlockSpec(memory_space=pl.ANY)],
        out_specs=[hblk, rows8(cfg.d_inner)],
        out_shape=[jax.ShapeDtypeStruct(h0.shape, F32), jax.ShapeDtypeStruct((tm, cfg.d_inner), F32)],
        input_output_aliases={5: 1},
        compiler_params=_cparams(("arbitrary",), blk, 8 << 20),
        name="ssd_state",
    )(dec_sh, h0, co, bo, xsw, jnp.zeros((tm, cfg.d_inner), F32))
    blk = 3 * _nbytes((tm, gw), F32) + 2 * _nbytes((tm, gw), BF16)
    yg = pl.pallas_call(
        _ssd_post_kernel,
        grid=(g_,),
        in_specs=[own(gw), own(gw), own(gw), row(gw, lambda g: (sb, zb0 + g)), par(1, gw, lambda g: (0, g))],
        out_specs=own(gw),
        out_shape=jax.ShapeDtypeStruct((tm, cfg.d_inner), BF16),
        compiler_params=_cparams(("parallel",), blk, 4 * _nbytes((tm, gw), F32)),
        name="ssd_post",
    )(ypre, yoff, ecs, proj, prm["gnorm_w"])
    return yg, hout


def _flat_state(cfg, buf):
    db, km1, c = buf.shape
    e = jnp.pad(buf.astype(F32), ((0, 0), (0, cfg.ds - km1), (0, 0))).reshape(db * cfg.ds, c)
    return jnp.pad(e, ((0, cfg.tm - db * cfg.ds), (0, 0)))


def _layer_params(cfg, w_in, sconv_w, wa_out, mconv_w, mconv_b, dt_bias, a_log, d_skip, gnorm_w, wb_out, w_o,
                  w_up, fconv_w, fconv_b, w_down):
    g_, hpg, d_ff, dfp = N_GROUPS, cfg.hpg, cfg.d_ff, cfg.dfp
    grp = lambda v: jnp.pad(v.astype(F32).reshape(g_, 1, hpg), ((0, 0), (0, 0), (0, LANES - hpg)))
    w_dt = w_in[:, cfg.nmain:].reshape(cfg.d, g_, hpg)
    w_dt = jnp.pad(w_dt, ((0, 0), (0, 0), (0, LANES - hpg))).reshape(cfg.d, g_ * LANES)
    padf = lambda a: jnp.pad(a, [(0, 0)] * (a.ndim - 1) + [(0, dfp - d_ff)])
    head = jnp.arange(LANES)[:, None]
    col = jnp.arange(cfg.gw)[None, :] // HEAD_DIM
    return dict(
        w_in=w_in.astype(BF16), w_dt=w_dt.astype(BF16), sconv_w=sconv_w.astype(F32),
        wa_out=wa_out.astype(BF16), wb_out=wb_out.astype(BF16), w_o=w_o.astype(BF16),
        mconv_w=mconv_w.astype(F32), mconv_b=mconv_b.astype(F32).reshape(1, -1),
        dt_bias_g=grp(dt_bias), a_log_g=grp(a_log),
        d_skip_e=jnp.repeat(d_skip.astype(F32), HEAD_DIM).reshape(1, -1),
        gnorm_w=gnorm_w.astype(F32).reshape(1, -1),
        w_up=jnp.concatenate([padf(w_up[:, :d_ff]), padf(w_up[:, d_ff:])], axis=1).astype(BF16),
        fconv_w=jnp.stack([padf(fconv_w[:, :d_ff]), padf(fconv_w[:, d_ff:])]).astype(F32),
        fconv_b=jnp.stack([padf(fconv_b[:d_ff]), padf(fconv_b[d_ff:])]).astype(F32).reshape(2, 1, dfp),
        w_down=jnp.pad(w_down, ((0, dfp - d_ff), (0, 0))).astype(BF16),
        expand=((head == col) & (head < hpg)).astype(BF16),
        tril=(jnp.arange(SSD_CHUNK)[:, None] >= jnp.arange(SSD_CHUNK)[None, :]).astype(BF16),
    )


def _layer(cfg, x, xn, prm, st_sc, st_mc, st_h, st_fc, next_g, next_dtype):
    d, d_ff, dfp = cfg.d, cfg.d_ff, cfg.dfp
    nrow = cfg.db * cfg.ds
    proj = _mm_fullk(cfg, xn, prm["w_in"], cfg.nmain, 0, BF16, name="proj")
    dtr_g = _mm_fullk(cfg, xn, prm["w_dt"], N_GROUPS * LANES, 0, F32, name="proj_dt")
    v, p_sc, s_sc = _sconv(cfg, proj, _flat_state(cfg, st_sc), prm["sconv_w"])
    out_a = _mm_fulln(cfg, v, prm["wa_out"], name="wa_out")
    yg, p_h, p_mc = _ssd_prompt(cfg, proj, dtr_g, prm)
    yg_s, s_h = _ssd_sample(cfg, proj, dtr_g, _flat_state(cfg, st_mc), st_h, prm)
    yg = lax.dynamic_update_slice(yg, yg_s, (cfg.sb * cfg.tm, 0))
    out_b = _mm_fulln(cfg, yg, prm["wb_out"], name="wb_out")
    xbc_s = proj[cfg.sb * cfg.tm:cfg.sb * cfg.tm + nrow, 7 * d:7 * d + cfg.conv_dim]
    s_mc = xbc_s.astype(F32).reshape(cfg.db, cfg.ds, cfg.conv_dim)[:, cfg.ds - 3:]
    mix = _mix(cfg, proj, out_a, out_b)
    x, xn2 = _add_rmsnorm(cfg, x, _mm_fulln(cfg, mix, prm["w_o"], name="w_o"), prm["norm2_g"], BF16)
    up = _mm_fullk(cfg, xn2, prm["w_up"], 2 * dfp, 0, BF16, name="w_up")
    fc_pad = jnp.concatenate([jnp.pad(st_fc[..., :d_ff], ((0, 0), (0, 0), (0, dfp - d_ff))),
                              jnp.pad(st_fc[..., d_ff:], ((0, 0), (0, 0), (0, dfp - d_ff)))], axis=-1)
    hmid, p_fc4 = _ffn_gate(cfg, up, _flat_state(cfg, fc_pad), prm["fconv_w"], prm["fconv_b"])
    p_fc = jnp.concatenate([p_fc4[:, 0, :, :d_ff], p_fc4[:, 1, :, :d_ff]], axis=-1)
    up_s = up[cfg.sb * cfg.tm:cfg.sb * cfg.tm + nrow].astype(F32).reshape(cfg.db, cfg.ds, 2 * dfp)[:, cfg.ds - 2:]
    s_fc = jnp.concatenate([up_s[..., :d_ff], up_s[..., dfp:dfp + d_ff]], axis=-1)
    x, xn_next = _add_rmsnorm(cfg, x, _mm_fulln(cfg, hmid, prm["w_down"], name="w_down"), next_g, next_dtype)
    return x, xn_next, (p_sc, p_mc, p_h, p_fc, s_sc, s_mc, s_h, s_fc)


def kernel(x_prompt, x_sample, state_sconv, state_mconv, state_ssm, state_fconv, meta_tokens, norm1_g, w_in,
           sconv_w, wa_out, mconv_w, mconv_b, dt_bias, a_log, d_skip, gnorm_w, wb_out, w_o, norm2_g, w_up,
           fconv_w, fconv_b, w_down, final_g):
    cfg = _make_cfg(x_prompt, x_sample, state_ssm, w_up)
    depth = w_in.shape[0]
    dtype = x_prompt.dtype
    nrow = cfg.db * cfg.ds
    meta = jnp.broadcast_to(meta_tokens.astype(dtype)[None], (cfg.bp, N_META, cfg.d))
    x = jnp.concatenate([
        jnp.concatenate([meta, x_prompt], axis=1).reshape(cfg.bp * cfg.lp, cfg.d),
        x_sample.reshape(nrow, cfg.d),
        jnp.zeros((cfg.r - cfg.bp * cfg.lp - nrow, cfg.d), dtype)], axis=0).astype(F32)
    xn = _rmsnorm(cfg, x, norm1_g[0], BF16)
    states = []
    for l in range(depth):
        prm = _layer_params(cfg, w_in[l], sconv_w[l], wa_out[l], mconv_w[l], mconv_b[l], dt_bias[l], a_log[l],
                            d_skip[l], gnorm_w[l], wb_out[l], w_o[l], w_up[l], fconv_w[l], fconv_b[l], w_down[l])
        prm["norm2_g"] = norm2_g[l]
        last = l == depth - 1
        x, xn, st = _layer(cfg, x, xn, prm, state_sconv[l], state_mconv[l], state_ssm[l], state_fconv[l],
                           final_g if last else norm1_g[l + 1], F32 if last else BF16)
        states.append(st)
    y_prompt = xn[:cfg.bp * cfg.lp].reshape(cfg.bp, cfg.lp, cfg.d)[:, N_META:].astype(dtype)
    y_sample = xn[cfg.sb * cfg.tm:cfg.sb * cfg.tm + nrow].reshape(cfg.db, cfg.ds, cfg.d).astype(dtype)
    stacked = [jnp.stack([s[i] for s in states]).astype(dtype) for i in range(8)]
    return (y_prompt, y_sample, *stacked)
```

```python
import functools
import math
from typing import NamedTuple

import jax
import jax.numpy as jnp
from jax import lax
from jax.experimental import pallas as pl
from jax.experimental.pallas import tpu as pltpu

F32 = jnp.float32
BF16 = jnp.bfloat16

N_META = 16
N_GROUPS = 8
HEAD_DIM = 64
SSD_CHUNK = 128
EPS = 1e-6

LANES = 128
SUBLANES = 8
BF16_ROWS = 16
VMEM_PHYSICAL_BYTES = 64 * 1024 * 1024
VMEM_BUDGET_BYTES = 56 * 1024 * 1024
COMPILER_SCRATCH_BYTES = 6 * 1024 * 1024
NORM_ROWS = 256
MM_VMEM_BYTES = 48 * 1024 * 1024
KSPLIT_MAX_K = 6144


class Cfg(NamedTuple):
    d: int
    bp: int
    seq: int
    lp: int
    db: int
    ds: int
    h: int
    hpg: int
    gw: int
    n: int
    d_inner: int
    conv_dim: int
    d_ff: int
    nt: int
    tm: int
    sb: int
    r: int
    nmain: int


def _largest_tile(dim, target, quantum=LANES):
    best = None
    t = quantum
    while t <= min(dim, target):
        if dim % t == 0:
            best = t
        t += quantum
    assert best is not None, (dim, target, quantum)
    return best


def _make_cfg(x_prompt, x_sample, state_ssm, w_up):
    bp, seq, d = x_prompt.shape
    db, ds, _ = x_sample.shape
    _, _, h, p, n = state_ssm.shape
    assert p == HEAD_DIM and h % N_GROUPS == 0 and seq % SSD_CHUNK == 0
    hpg = h // N_GROUPS
    assert hpg % 2 == 0 and hpg <= LANES
    d_inner = h * HEAD_DIM
    lp = N_META + seq
    d_ff = w_up.shape[-1] // 2
    nt = 3 if (lp % (3 * BF16_ROWS) == 0 and lp // 3 >= db * ds) else 1
    tm = lp // nt
    assert tm % BF16_ROWS == 0 and db * ds <= tm and ds >= 3
    sb = bp * nt
    return Cfg(d=d, bp=bp, seq=seq, lp=lp, db=db, ds=ds, h=h, hpg=hpg, gw=hpg * HEAD_DIM, n=n,
               d_inner=d_inner, conv_dim=d_inner + 2 * N_GROUPS * n, d_ff=d_ff, nt=nt, tm=tm,
               sb=sb, r=(sb + 1) * tm, nmain=5 * d + d_inner + d_inner + 2 * N_GROUPS * n)


def _cparams(semantics, block_bytes, temp_bytes=0):
    need = 2 * block_bytes + temp_bytes + COMPILER_SCRATCH_BYTES
    return pltpu.CompilerParams(dimension_semantics=semantics,
                                vmem_limit_bytes=int(min(max(need, 16 << 20), VMEM_BUDGET_BYTES)))


def _nbytes(shape, dtype):
    return math.prod(shape) * jnp.dtype(dtype).itemsize


def _silu(x):
    half = 0.5 * x
    return half + half * jnp.tanh(half)


def _softplus(x):
    return jnp.maximum(x, 0.0) + jnp.log1p(jnp.exp(-jnp.abs(x)))


def _shift_down(x, s):
    return x if s == 0 else pltpu.roll(x, s, axis=0)


def _shift_up(x, s):
    return x if s == 0 else pltpu.roll(x, x.shape[0] - s, axis=0)


def _rows(shape):
    return lax.broadcasted_iota(jnp.int32, shape, 0)


def _dot(a, b):
    return jnp.dot(a, b, preferred_element_type=F32)


def _dot_nt(a, b):
    return lax.dot_general(a, b, (((1,), (1,)), ((), ())), preferred_element_type=F32)


def _expand_heads(v, e):
    hi = v.astype(BF16)
    lo = (v - hi.astype(F32)).astype(BF16)
    return _dot(hi, e) + _dot(lo, e)


def _rmsnorm_kernel(x_ref, g_ref, o_ref):
    x = x_ref[...]
    y = x * lax.rsqrt(jnp.mean(x * x, axis=-1, keepdims=True) + EPS)
    o_ref[...] = (y * g_ref[...]).astype(o_ref.dtype)


def _rmsnorm(cfg, x, g, out_dtype):
    tr = _largest_tile(cfg.r, NORM_ROWS, BF16_ROWS)
    blk = _nbytes((tr, cfg.d), F32) + _nbytes((tr, cfg.d), out_dtype)
    return pl.pallas_call(
        _rmsnorm_kernel,
        grid=(cfg.r // tr,),
        in_specs=[pl.BlockSpec((tr, cfg.d), lambda i: (i, 0)),
                  pl.BlockSpec((1, cfg.d), lambda i: (0, 0))],
        out_specs=pl.BlockSpec((tr, cfg.d), lambda i: (i, 0)),
        out_shape=jax.ShapeDtypeStruct((cfg.r, cfg.d), out_dtype),
        compiler_params=_cparams(("parallel",), blk, _nbytes((tr, cfg.d), F32)),
        name="rmsnorm",
    )(x, g.reshape(1, cfg.d))


def _mm_kernel(x_ref, w_ref, *refs, epilogue):
    o_ref = refs[-1]
    acc = _dot(x_ref[...], w_ref[...])
    if epilogue == "plain":
        o_ref[...] = acc.astype(o_ref.dtype)
    elif epilogue == "resid":
        o_ref[...] = refs[0][...] + acc
    elif epilogue == "mix":
        ga_ref, gb_ref, a_ref = refs[:3]
        ga = jax.nn.sigmoid(ga_ref[...].astype(F32))
        gb = jax.nn.sigmoid(gb_ref[...].astype(F32))
        o_ref[...] = (ga * a_ref[...].astype(F32) + gb * acc).astype(o_ref.dtype)
    else:
        raise ValueError(epilogue)


def _mm(cfg, x, w, layer, ncols, out_dtype, name, epilogue="plain", extras=()):
    k = x.shape[1]
    per_elem = jnp.dtype(out_dtype).itemsize + sum(jnp.dtype(dt).itemsize for _, _, dt in extras)

    def need(tn):
        return 2 * (_nbytes((cfg.tm, k), BF16) + _nbytes((k, tn), BF16) + cfg.tm * tn * per_elem)

    tn = next(t for t in (1024, 512, 256, LANES)
              if ncols % t == 0 and need(t) + 2 * _nbytes((cfg.tm, t), F32) <= MM_VMEM_BYTES)
    nb = ncols // tn
    tile = lambda off: pl.BlockSpec((cfg.tm, tn), lambda i, j: (i, j + off * nb))
    return pl.pallas_call(
        functools.partial(_mm_kernel, epilogue=epilogue),
        grid=(cfg.r // cfg.tm, nb),
        in_specs=[pl.BlockSpec((cfg.tm, k), lambda i, j: (i, 0)),
                  pl.BlockSpec((None, k, tn), lambda i, j: (layer, 0, j))]
                 + [tile(off) for _, off, _ in extras],
        out_specs=tile(0),
        out_shape=jax.ShapeDtypeStruct((cfg.r, ncols), out_dtype),
        compiler_params=_cparams(("parallel", "arbitrary"), need(tn) // 2, 2 * _nbytes((cfg.tm, tn), F32)),
        name=name,
    )(x, w, *[a for a, _, _ in extras])


def _mm_ksplit_kernel(x_ref, w_ref, r_ref, o_ref):
    k = pl.program_id(2)
    part = _dot(x_ref[...], w_ref[...])

    @pl.when(k == 0)
    def _():
        o_ref[...] = r_ref[...] + part

    @pl.when(k > 0)
    def _():
        o_ref[...] += part


def _mm_ksplit(cfg, x, w, layer, res, name):
    _, k, n = w.shape
    tk = _largest_tile(k, KSPLIT_MAX_K)
    tn = _largest_tile(n, 512)
    blk = (_nbytes((cfg.tm, tk), BF16) + _nbytes((tk, tn), BF16) + 2 * _nbytes((cfg.tm, tn), F32))
    return pl.pallas_call(
        _mm_ksplit_kernel,
        grid=(cfg.r // cfg.tm, n // tn, k // tk),
        in_specs=[pl.BlockSpec((cfg.tm, tk), lambda i, j, kk: (i, kk)),
                  pl.BlockSpec((None, tk, tn), lambda i, j, kk: (layer, kk, j)),
                  pl.BlockSpec((cfg.tm, tn), lambda i, j, kk: (i, j))],
        out_specs=pl.BlockSpec((cfg.tm, tn), lambda i, j, kk: (i, j)),
        out_shape=jax.ShapeDtypeStruct((cfg.r, n), F32),
        compiler_params=_cparams(("parallel", "parallel", "arbitrary"), blk, 2 * _nbytes((cfg.tm, tn), F32)),
        name=name,
    )(x, w, res)


def _prompt_taps(x, k):
    row = _rows(x.shape)
    return [x] + [jnp.where(row >= s, _shift_down(x, s), 0.0) for s in range(1, k)]


def _sample_taps(x, e, k, ds):
    t = _rows(x.shape) % ds
    return [x] + [jnp.where(t >= s, _shift_down(x, s), _shift_up(e, k - 1 - s)) for s in range(1, k)]


def _conv_from_taps(taps, w, b=None):
    k = len(taps)
    y = taps[0] * w[k - 1:k]
    for s in range(1, k):
        y = y + taps[s] * w[k - 1 - s:k - s]
    return y if b is None else y + b


def _sconv_prompt_kernel(h_ref, b_ref, c_ref, w_ref, v_ref, st_ref, *, lp):
    ch = c_ref[...].astype(F32) * h_ref[...].astype(F32)
    u = _conv_from_taps(_prompt_taps(ch, 3), w_ref[...])
    v_ref[...] = (b_ref[...].astype(F32) * u).astype(v_ref.dtype)
    st_ref[0] = ch[lp - SUBLANES:lp][SUBLANES - 2:]


def _sconv_sample_kernel(h_ref, b_ref, c_ref, e_ref, w_ref, v_ref, ch_ref, *, ds):
    ch = c_ref[...].astype(F32) * h_ref[...].astype(F32)
    u = _conv_from_taps(_sample_taps(ch, e_ref[...], 3, ds), w_ref[...])
    v_ref[...] = (b_ref[...].astype(F32) * u).astype(v_ref.dtype)
    ch_ref[...] = ch


def _sconv(cfg, proj, e_sc, w):
    d = cfg.d
    tc = _largest_tile(d, 512)
    nb = d // tc
    pblk = lambda off: pl.BlockSpec((cfg.lp, tc), lambda b, j: (b, j + off * nb))
    blk = 4 * _nbytes((cfg.lp, tc), BF16)
    v_p, st_p = pl.pallas_call(
        functools.partial(_sconv_prompt_kernel, lp=cfg.lp),
        grid=(cfg.bp, nb),
        in_specs=[pblk(2), pblk(3), pblk(4), pl.BlockSpec((3, tc), lambda b, j: (0, j))],
        out_specs=[pl.BlockSpec((cfg.lp, tc), lambda b, j: (b, j)),
                   pl.BlockSpec((1, 2, tc), lambda b, j: (b, 0, j))],
        out_shape=[jax.ShapeDtypeStruct((cfg.r, d), BF16),
                   jax.ShapeDtypeStruct((cfg.bp, 2, d), F32)],
        compiler_params=_cparams(("parallel", "parallel"), blk, 6 * _nbytes((cfg.lp, tc), F32)),
        name="sconv_prompt",
    )(proj, proj, proj, w)
    sblk = lambda off: pl.BlockSpec((cfg.tm, tc), lambda j: (cfg.sb, j + off * nb))
    own = pl.BlockSpec((cfg.tm, tc), lambda j: (0, j))
    blk = 4 * _nbytes((cfg.tm, tc), BF16) + 2 * _nbytes((cfg.tm, tc), F32)
    v_s, ch_s = pl.pallas_call(
        functools.partial(_sconv_sample_kernel, ds=cfg.ds),
        grid=(nb,),
        in_specs=[sblk(2), sblk(3), sblk(4), own, pl.BlockSpec((3, tc), lambda j: (0, j))],
        out_specs=[own, own],
        out_shape=[jax.ShapeDtypeStruct((cfg.tm, d), BF16),
                   jax.ShapeDtypeStruct((cfg.tm, d), F32)],
        compiler_params=_cparams(("parallel",), blk, 6 * _nbytes((cfg.tm, tc), F32)),
        name="sconv_sample",
    )(proj, proj, proj, e_sc, w)
    v = lax.dynamic_update_slice(v_p, v_s, (cfg.sb * cfg.tm, 0))
    st_s = ch_s[:cfg.db * cfg.ds].reshape(cfg.db, cfg.ds, d)[:, cfg.ds - 2:]
    return v, st_p, st_s


def _tile_taps(x, halo, k):
    row8 = _rows((SUBLANES, x.shape[1]))
    taps = [x]
    for s in range(1, k):
        body = _shift_down(x, s)
        top = jnp.where(row8 >= s, body[:SUBLANES], _shift_down(halo, s))
        taps.append(jnp.concatenate([top, body[SUBLANES:]], axis=0))
    return taps


def _ffn_up_kernel(x_ref, wa_ref, wg_ref, ea_ref, eg_ref, cwa_ref, cwg_ref, cba_ref, cbg_ref,
                   h_ref, pst_ref, ups_ref, raw_ref, halo_ref, *, cfg):
    i = pl.program_id(0)
    j = pl.program_id(1)
    tm = cfg.tm

    @pl.when(jnp.logical_and(i == 0, j == 0))
    def _():
        raw_ref[...] = jnp.zeros_like(raw_ref)
        halo_ref[...] = jnp.zeros_like(halo_ref)

    def variant(sample):
        x = x_ref[...]
        raw_ref[0, SUBLANES:, :] = _dot(x, wa_ref[...])
        raw_ref[1, SUBLANES:, :] = _dot(x, wg_ref[...])
        lag = lambda half, s: raw_ref[half, pl.ds(SUBLANES - s, tm), :]
        if sample:
            t = _rows((tm, 1)) % cfg.ds
            taps = [[lag(half, 0)] + [jnp.where(t >= s, lag(half, s), e_ref[pl.ds(2 - s, tm), :]) for s in (1, 2)]
                    for half, e_ref in ((0, ea_ref), (1, eg_ref))]
            ups_ref[0] = taps[0][0]
            ups_ref[1] = taps[1][0]
        else:
            inside = i % cfg.nt != 0
            for half in (0, 1):
                raw_ref[half, :SUBLANES, :] = jnp.where(inside, halo_ref[j, half], 0.0)
            taps = [[lag(half, s) for s in range(3)] for half in (0, 1)]
        act = _conv_from_taps(taps[0], cwa_ref[...], cba_ref[...])
        gate = _conv_from_taps(taps[1], cwg_ref[...], cbg_ref[...])
        h_ref[...] = (_silu(act) * gate).astype(h_ref.dtype)
        for half in (0, 1):
            tail = raw_ref[half, tm:, :]
            pst_ref[0, half] = tail[SUBLANES - 2:]
            halo_ref[j, half] = tail

    pl.when(i < cfg.sb)(functools.partial(variant, False))
    pl.when(i == cfg.sb)(functools.partial(variant, True))


def _ffn_up(cfg, xn, w_up, layer, e_fc, cw, cb):
    d_ff, tm, sb, nt = cfg.d_ff, cfg.tm, cfg.sb, cfg.nt
    k = xn.shape[1]
    tn = _largest_tile(d_ff, 512)
    nb = d_ff // tn
    only_sb = lambda i, j: jnp.where(i == sb, j, 0)
    blk = (_nbytes((tm, k), BF16) + 2 * _nbytes((k, tn), BF16) + 2 * _nbytes((tm, tn), F32)
           + _nbytes((tm, tn), BF16) + 2 * _nbytes((tm, tn), F32))
    h, pst, ups = pl.pallas_call(
        functools.partial(_ffn_up_kernel, cfg=cfg),
        grid=(cfg.r // tm, nb),
        in_specs=[pl.BlockSpec((tm, k), lambda i, j: (i, 0)),
                  pl.BlockSpec((None, k, tn), lambda i, j: (layer, 0, j)),
                  pl.BlockSpec((None, k, tn), lambda i, j: (layer, 0, j + nb)),
                  pl.BlockSpec((tm + SUBLANES, tn), lambda i, j: (0, only_sb(i, j))),
                  pl.BlockSpec((tm + SUBLANES, tn), lambda i, j: (0, only_sb(i, j) + nb)),
                  pl.BlockSpec((3, tn), lambda i, j: (0, j)), pl.BlockSpec((3, tn), lambda i, j: (0, j + nb)),
                  pl.BlockSpec((1, tn), lambda i, j: (0, j)), pl.BlockSpec((1, tn), lambda i, j: (0, j + nb))],
        out_specs=[pl.BlockSpec((tm, tn), lambda i, j: (i, j)),
                   pl.BlockSpec((1, 2, 2, tn), lambda i, j: (i, 0, 0, j)),
                   pl.BlockSpec((2, tm, tn), lambda i, j: (0, 0, only_sb(i, j)))],
        out_shape=[jax.ShapeDtypeStruct((cfg.r, d_ff), BF16),
                   jax.ShapeDtypeStruct((cfg.r // tm, 2, 2, d_ff), F32),
                   jax.ShapeDtypeStruct((2, tm, d_ff), F32)],
        scratch_shapes=[pltpu.VMEM((2, SUBLANES + tm, tn), F32), pltpu.VMEM((nb, 2, SUBLANES, tn), F32)],
        compiler_params=_cparams(("arbitrary", "arbitrary"), blk, 16 * _nbytes((tm, tn), F32)),
        name="ffn_up",
    )(xn, w_up, w_up, e_fc, e_fc, cw, cw, cb, cb)
    last = pst[nt - 1:sb:nt]
    p_fc = jnp.concatenate([last[:, 0], last[:, 1]], axis=-1)
    return h, p_fc, ups


def _ssd_gate_norm(y, z, gnw):
    y = y * _silu(z)
    return y * lax.rsqrt(jnp.mean(y * y, axis=-1, keepdims=True) + EPS) * gnw


def _ssd_prompt_kernel(x_ref, bm_ref, cm_ref, z_ref, dtr_ref, wx_ref, wb_ref, wc_ref, bx_ref, bb_ref, bc_ref,
                       dtb_ref, alog_ref, dsk_ref, gnw_ref, e_ref, tril_ref,
                       y_ref, hfin_ref, mcx_ref, mcb_ref, mcc_ref,
                       ht_ref, hx_ref, hb_ref, hc_ref, *, cfg):
    lp, hpg, n, gw = cfg.lp, cfg.hpg, cfg.n, cfg.gw
    c = SSD_CHUNK
    nch = 1 + cfg.seq // c
    ht_ref[...] = jnp.zeros_like(ht_ref)
    hx_ref[...] = jnp.zeros_like(hx_ref)
    hb_ref[...] = jnp.zeros_like(hb_ref)
    hc_ref[...] = jnp.zeros_like(hc_ref)
    a_neg = -jnp.exp(alog_ref[0])
    dtb = dtb_ref[0]
    e = e_ref[...]
    tril = tril_ref[...]
    li = lax.broadcasted_iota(jnp.int32, (c, c), 0)
    si = lax.broadcasted_iota(jnp.int32, (c, c), 1)
    causal = li >= si
    lane_lo = lax.broadcasted_iota(jnp.int32, (c, 2 * HEAD_DIM), 1) < HEAD_DIM

    def conv_silu(src_ref, rows, buf_ref, w_ref, b_ref, first):
        buf_ref[SUBLANES:, :] = src_ref[rows, :].astype(F32)
        y = _silu(_conv_from_taps([buf_ref[pl.ds(SUBLANES - s, c), :] for s in range(4)], w_ref[...], b_ref[...]))
        buf_ref[:SUBLANES, :] = jnp.where(first, buf_ref[N_META:N_META + SUBLANES, :], buf_ref[c:, :])
        return y

    def chunk(k, carry):
        first = k == 0
        start = pl.multiple_of(jnp.where(first, 0, N_META + c * (k - 1)), BF16_ROWS)
        rows = pl.ds(start, c)
        xh = conv_silu(x_ref, rows, hx_ref, wx_ref, bx_ref, first)
        bc = conv_silu(bm_ref, rows, hb_ref, wb_ref, bb_ref, first)
        cc = conv_silu(cm_ref, rows, hc_ref, wc_ref, bc_ref, first)
        live = jnp.logical_or(jnp.logical_not(first), _rows((c, LANES)) < N_META)
        dt = jnp.where(live, _softplus(dtr_ref[rows, :] + dtb), 0.0)
        a = dt * a_neg
        a1 = a.astype(BF16)
        r1 = a - a1.astype(F32)
        a2 = r1.astype(BF16)
        a3 = (r1 - a2.astype(F32)).astype(BF16)
        cs = _dot(tril, a1) + _dot(tril, a2) + _dot(tril, a3)
        cs_t = cs.T
        cs_last = cs[c - 1:c]
        ecs = jnp.exp(cs)
        both_e = _dot(jnp.concatenate([dt, dt * jnp.exp(cs_last - cs)], axis=0).astype(BF16), e)
        xs_b = (xh * both_e[:c]).astype(BF16)
        xsw_b = (xh * both_e[c:]).astype(BF16)
        dec_row = _expand_heads(jnp.broadcast_to(ecs[c - 1:c], (SUBLANES, LANES)), e)[:1]
        scores = jnp.where(causal, _dot_nt(cc.astype(BF16), bc.astype(BF16)), 0.0)
        ht = ht_ref[...]
        ht_b = ht.astype(BF16)
        pairs = []
        for q in range(hpg // 2):
            cols = slice(2 * HEAD_DIM * q, 2 * HEAD_DIM * (q + 1))
            rhs = jnp.concatenate([xs_b[:, cols], ht_b[:, cols]], axis=0)
            res = []
            for r in (2 * q, 2 * q + 1):
                m = scores * jnp.exp(jnp.minimum(cs[:, r:r + 1] - cs_t[r:r + 1, :], 0.0))
                lhs = jnp.concatenate([m, cc * ecs[:, r:r + 1]], axis=1).astype(BF16)
                res.append(_dot(lhs, rhs))
            pairs.append(jnp.where(lane_lo, res[0], res[1]))
        y = jnp.concatenate(pairs, axis=1) if len(pairs) > 1 else pairs[0]
        ht_ref[...] = ht * dec_row + _dot(bc.T.astype(BF16), xsw_b)
        y = y + dsk_ref[...] * xh
        y_ref[rows, :] = _ssd_gate_norm(y, z_ref[rows, :].astype(F32), gnw_ref[...]).astype(y_ref.dtype)
        return carry

    lax.fori_loop(0, nch, chunk, 0)
    hfin_ref[0] = ht_ref[...].T.reshape(hpg, HEAD_DIM, n)
    tail = lambda ref: ref[lp - BF16_ROWS:lp, :].astype(F32)[BF16_ROWS - 3:]
    mcx_ref[0] = tail(x_ref)
    mcb_ref[0] = tail(bm_ref)
    mcc_ref[0] = tail(cm_ref)


def _ssd_prompt(cfg, proj, dtr_g, prm):
    lp, gw, n, hpg, g_ = cfg.lp, cfg.gw, cfg.n, cfg.hpg, N_GROUPS
    xb0 = 7 * cfg.d // gw
    zb0 = 5 * cfg.d // gw
    bb0 = 9 * cfg.d // n
    cvb = cfg.d_inner // n
    colblk = lambda w, f: pl.BlockSpec((lp, w), f)
    par = lambda rows, w, f: pl.BlockSpec((rows, w), f)
    vec3 = pl.BlockSpec((1, 1, LANES), lambda b, g: (g, 0, 0))
    in_specs = [
        colblk(gw, lambda b, g: (b, xb0 + g)), colblk(n, lambda b, g: (b, bb0 + g)),
        colblk(n, lambda b, g: (b, bb0 + g_ + g)), colblk(gw, lambda b, g: (b, zb0 + g)),
        colblk(LANES, lambda b, g: (b, g)),
        par(4, gw, lambda b, g: (0, g)), par(4, n, lambda b, g: (0, cvb + g)), par(4, n, lambda b, g: (0, cvb + g_ + g)),
        par(1, gw, lambda b, g: (0, g)), par(1, n, lambda b, g: (0, cvb + g)), par(1, n, lambda b, g: (0, cvb + g_ + g)),
        vec3, vec3,
        par(1, gw, lambda b, g: (0, g)), par(1, gw, lambda b, g: (0, g)),
        par(LANES, gw, lambda b, g: (0, 0)), par(SSD_CHUNK, SSD_CHUNK, lambda b, g: (0, 0)),
    ]
    out_specs = [
        pl.BlockSpec((lp, gw), lambda b, g: (b, g)),
        pl.BlockSpec((1, hpg, HEAD_DIM, n), lambda b, g: (b, g, 0, 0)),
        pl.BlockSpec((1, 3, gw), lambda b, g: (b, 0, g)),
        pl.BlockSpec((1, 3, n), lambda b, g: (b, 0, g)),
        pl.BlockSpec((1, 3, n), lambda b, g: (b, 0, g)),
    ]
    out_shape = [
        jax.ShapeDtypeStruct((cfg.r, cfg.d_inner), BF16),
        jax.ShapeDtypeStruct((cfg.bp, cfg.h, HEAD_DIM, n), F32),
        jax.ShapeDtypeStruct((cfg.bp, 3, cfg.d_inner), F32),
        jax.ShapeDtypeStruct((cfg.bp, 3, g_ * n), F32),
        jax.ShapeDtypeStruct((cfg.bp, 3, g_ * n), F32),
    ]
    blk = 3 * _nbytes((lp, gw), BF16) + 2 * _nbytes((lp, n), BF16) + _nbytes((lp, LANES), F32)
    y, hfin, mcx, mcb, mcc = pl.pallas_call(
        functools.partial(_ssd_prompt_kernel, cfg=cfg),
        grid=(cfg.bp, g_),
        in_specs=in_specs, out_specs=out_specs, out_shape=out_shape,
        scratch_shapes=[pltpu.VMEM((n, gw), F32), pltpu.VMEM((SUBLANES + SSD_CHUNK, gw), F32),
                        pltpu.VMEM((SUBLANES + SSD_CHUNK, n), F32), pltpu.VMEM((SUBLANES + SSD_CHUNK, n), F32)],
        compiler_params=_cparams(("parallel", "parallel"), blk, 24 * _nbytes((SSD_CHUNK, gw), F32) + (4 << 20)),
        name="ssd_prompt",
    )(proj, proj, proj, proj, dtr_g,
      prm["mconv_w"], prm["mconv_w"], prm["mconv_w"], prm["mconv_b"], prm["mconv_b"], prm["mconv_b"],
      prm["dt_bias_g"], prm["a_log_g"], prm["d_skip_e"], prm["gnorm_w"], prm["expand"], prm["tril"])
    return y, hfin, jnp.concatenate([mcx, mcb, mcc], axis=-1)


def _ssd_tok_kernel(x_ref, bm_ref, cm_ref, dtr_ref, ex_ref, eb_ref, ec_ref, wx_ref, wb_ref, wc_ref,
                    bx_ref, bb_ref, bc_ref, dtb_ref, alog_ref, dsk_ref, e_ref,
                    ypre_ref, ecs_ref, xsw_ref, bo_ref, co_ref, dec_ref, *, ds):
    e = e_ref[...]
    conv = lambda ref, eref, w, b: _silu(_conv_from_taps(
        _sample_taps(ref[...].astype(F32), eref[...], 4, ds), w[...], b[...]))
    xh = conv(x_ref, ex_ref, wx_ref, bx_ref)
    bc = conv(bm_ref, eb_ref, wb_ref, bb_ref)
    cc = conv(cm_ref, ec_ref, wc_ref, bc_ref)
    dt = _softplus(dtr_ref[...] + dtb_ref[0])
    a = dt * (-jnp.exp(alog_ref[0]))
    t = _rows(a.shape) % ds
    t1 = _rows((a.shape[0], 1)) % ds
    cs = a
    for j in range(1, ds):
        cs = cs + jnp.where(t >= j, _shift_down(a, j), 0.0)
    cs_last = jnp.zeros_like(cs)
    for j in range(ds):
        cs_last = cs_last + jnp.where(t == ds - 1 - j, _shift_up(cs, j), 0.0)
    ecs_e = _expand_heads(jnp.exp(cs), e)
    xs = xh * _expand_heads(dt, e)
    y = dsk_ref[...] * xh
    for j in range(ds):
        sc = jnp.sum(cc * _shift_down(bc, j), axis=-1, keepdims=True)
        diff = jnp.where(t >= j, cs - _shift_down(cs, j), 0.0)
        coef = jnp.where(t1 >= j, sc, 0.0) * jnp.exp(diff)
        y = y + _expand_heads(coef, e) * _shift_down(xs, j)
    ypre_ref[...] = y
    ecs_ref[...] = ecs_e
    xsw_ref[...] = xs * _expand_heads(jnp.exp(cs_last - cs), e)
    bo_ref[...] = bc
    co_ref[...] = cc
    dec_ref[...] = jnp.exp(cs_last)


def _ssd_state_kernel(dec_ref, h0_ref, c_ref, b_ref, xsw_ref, *refs, cfg):
    hout_ref, yoff_ref = refs[-2:]
    hpg, n, gw, ds = cfg.hpg, cfg.n, cfg.gw, cfg.ds
    i = pl.program_id(0)
    nseq = SUBLANES // ds
    pad = jnp.zeros((LANES - SUBLANES, n), F32)
    row8 = _rows((SUBLANES, 1)) // ds
    cmat = c_ref[...]
    bmat = b_ref[...]
    xsw = xsw_ref[...]
    youts = []
    for g in range(N_GROUPS):
        cg = cmat[:, g * n:(g + 1) * n].astype(BF16)
        bg = bmat[:, g * n:(g + 1) * n]
        xw = jnp.concatenate([xsw[:, g * gw:(g + 1) * gw], jnp.zeros((LANES - SUBLANES, gw), F32)], axis=0)
        xw_t = xw.T.astype(BF16)
        yg = jnp.zeros((SUBLANES, gw), F32)
        for s in range(nseq):
            h0 = h0_ref[s, g * hpg:(g + 1) * hpg].reshape(gw, n)
            yg = jnp.where(row8 == s, _dot_nt(cg, h0.astype(BF16)), yg)
            bs = jnp.concatenate([jnp.where(row8 == s, bg, 0.0), pad], axis=0).astype(BF16)
            dh = _dot(xw_t, bs)
            for r in range(hpg):
                sl = slice(r * HEAD_DIM, (r + 1) * HEAD_DIM)
                hout_ref[s, g * hpg + r] = h0[sl] * dec_ref[i * nseq + s, g * hpg + r] + dh[sl]
        youts.append(yg)
    yoff_ref[...] = jnp.concatenate(youts, axis=1)


def _ssd_post_kernel(ypre_ref, yoff_ref, ecs_ref, z_ref, gnw_ref, o_ref):
    y = ypre_ref[...] + yoff_ref[...] * ecs_ref[...]
    o_ref[...] = _ssd_gate_norm(y, z_ref[...].astype(F32), gnw_ref[...]).astype(o_ref.dtype)


def _ssd_sample(cfg, proj, dtr_g, e_mc, h_all, layer, h_new, prm):
    tm, gw, n, g_, sb = cfg.tm, cfg.gw, cfg.n, N_GROUPS, cfg.sb
    xb0 = 7 * cfg.d // gw
    zb0 = 5 * cfg.d // gw
    bb0 = 9 * cfg.d // n
    cvb = cfg.d_inner // n
    row = lambda w, f: pl.BlockSpec((tm, w), f)
    par = lambda rows, w, f: pl.BlockSpec((rows, w), f)
    vec3 = pl.BlockSpec((1, 1, LANES), lambda g: (g, 0, 0))
    in_specs = [
        row(gw, lambda g: (sb, xb0 + g)), row(n, lambda g: (sb, bb0 + g)), row(n, lambda g: (sb, bb0 + g_ + g)),
        row(LANES, lambda g: (sb, g)),
        row(gw, lambda g: (0, g)), row(n, lambda g: (0, cvb + g)), row(n, lambda g: (0, cvb + g_ + g)),
        par(4, gw, lambda g: (0, g)), par(4, n, lambda g: (0, cvb + g)), par(4, n, lambda g: (0, cvb + g_ + g)),
        par(1, gw, lambda g: (0, g)), par(1, n, lambda g: (0, cvb + g)), par(1, n, lambda g: (0, cvb + g_ + g)),
        vec3, vec3, par(1, gw, lambda g: (0, g)), par(LANES, gw, lambda g: (0, 0)),
    ]
    own = lambda w: pl.BlockSpec((tm, w), lambda g: (0, g))
    blk = 2 * _nbytes((tm, gw), BF16) + 5 * _nbytes((tm, gw), F32)
    ypre, ecs, xsw, bo, co, dec = pl.pallas_call(
        functools.partial(_ssd_tok_kernel, ds=cfg.ds),
        grid=(g_,),
        in_specs=in_specs,
        out_specs=[own(gw), own(gw), own(gw), own(n), own(n), own(LANES)],
        out_shape=[jax.ShapeDtypeStruct((tm, cfg.d_inner), F32)] * 3
                  + [jax.ShapeDtypeStruct((tm, g_ * n), F32)] * 2
                  + [jax.ShapeDtypeStruct((tm, g_ * LANES), F32)],
        compiler_params=_cparams(("parallel",), blk, 16 * _nbytes((tm, gw), F32)),
        name="ssd_tok",
    )(proj, proj, proj, dtr_g, e_mc, e_mc, e_mc,
      prm["mconv_w"], prm["mconv_w"], prm["mconv_w"], prm["mconv_b"], prm["mconv_b"], prm["mconv_b"],
      prm["dt_bias_g"], prm["a_log_g"], prm["d_skip_e"], prm["expand"])
    nrow = cfg.db * cfg.ds
    dec_sh = dec[:nrow].reshape(cfg.db, cfg.ds, g_, LANES)[:, cfg.ds - 1, :, :cfg.hpg].reshape(cfg.db, cfg.h)
    nseq = SUBLANES // cfg.ds
    rows8 = lambda w: pl.BlockSpec((SUBLANES, w), lambda i: (i, 0))
    hblk = pl.BlockSpec((None, nseq, cfg.h, HEAD_DIM, n), lambda i: (layer, i, 0, 0, 0))
    blk = 2 * _nbytes((nseq, cfg.h, HEAD_DIM, n), F32) + 3 * _nbytes((SUBLANES, cfg.d_inner), F32)
    hbm = pl.BlockSpec(memory_space=pl.ANY)
    aliased = [jnp.zeros((tm, cfg.d_inner), F32)] + ([] if h_new is None else [h_new])
    hout, yoff = pl.pallas_call(
        functools.partial(_ssd_state_kernel, cfg=cfg),
        grid=(cfg.db // nseq,),
        in_specs=[pl.BlockSpec(memory_space=pltpu.SMEM), hblk, rows8(g_ * n), rows8(g_ * n), rows8(cfg.d_inner)]
                 + [hbm] * len(aliased),
        out_specs=[hblk, rows8(cfg.d_inner)],
        out_shape=[jax.ShapeDtypeStruct(h_all.shape, F32), jax.ShapeDtypeStruct((tm, cfg.d_inner), F32)],
        input_output_aliases={5: 1} if h_new is None else {5: 1, 6: 0},
        compiler_params=_cparams(("arbitrary",), blk, 8 << 20),
        name="ssd_state",
    )(dec_sh, h_all, co, bo, xsw, *aliased)
    blk = 3 * _nbytes((tm, gw), F32) + 2 * _nbytes((tm, gw), BF16)
    yg = pl.pallas_call(
        _ssd_post_kernel,
        grid=(g_,),
        in_specs=[own(gw), own(gw), own(gw), row(gw, lambda g: (sb, zb0 + g)), par(1, gw, lambda g: (0, g))],
        out_specs=own(gw),
        out_shape=jax.ShapeDtypeStruct((tm, cfg.d_inner), BF16),
        compiler_params=_cparams(("parallel",), blk, 4 * _nbytes((tm, gw), F32)),
        name="ssd_post",
    )(ypre, yoff, ecs, proj, prm["gnorm_w"])
    return yg, hout


def _flat_state(cfg, buf):
    db, km1, c = buf.shape
    e = jnp.pad(buf.astype(F32), ((0, 0), (0, cfg.ds - km1), (0, 0))).reshape(db * cfg.ds, c)
    return jnp.pad(e, ((0, cfg.tm + SUBLANES - db * cfg.ds), (0, 0)))


def _matrices(cfg, w_in, wa_out, wb_out, w_o, w_up, w_down):
    depth, g_, hpg = w_in.shape[0], N_GROUPS, cfg.hpg
    w_dt = w_in[:, :, cfg.nmain:].reshape(depth, cfg.d, g_, hpg)
    w_dt = jnp.pad(w_dt, ((0, 0), (0, 0), (0, 0), (0, LANES - hpg))).reshape(depth, cfg.d, g_ * LANES)
    return dict(w_in=w_in.astype(BF16), w_dt=w_dt.astype(BF16), wa_out=wa_out.astype(BF16),
                wb_out=wb_out.astype(BF16), w_o=w_o.astype(BF16), w_up=w_up.astype(BF16),
                w_down=w_down.astype(BF16))


def _layer_params(cfg, sconv_w, mconv_w, mconv_b, dt_bias, a_log, d_skip, gnorm_w, fconv_w, fconv_b):
    g_, hpg = N_GROUPS, cfg.hpg
    grp = lambda v: jnp.pad(v.astype(F32).reshape(g_, 1, hpg), ((0, 0), (0, 0), (0, LANES - hpg)))
    head = jnp.arange(LANES)[:, None]
    col = jnp.arange(cfg.gw)[None, :] // HEAD_DIM
    return dict(
        sconv_w=sconv_w.astype(F32),
        mconv_w=mconv_w.astype(F32), mconv_b=mconv_b.astype(F32).reshape(1, -1),
        dt_bias_g=grp(dt_bias), a_log_g=grp(a_log),
        d_skip_e=jnp.repeat(d_skip.astype(F32), HEAD_DIM).reshape(1, -1),
        gnorm_w=gnorm_w.astype(F32).reshape(1, -1),
        fconv_w=fconv_w.astype(F32), fconv_b=fconv_b.astype(F32).reshape(1, -1),
        expand=((head == col) & (head < hpg)).astype(BF16),
        tril=(jnp.arange(SSD_CHUNK)[:, None] >= jnp.arange(SSD_CHUNK)[None, :]).astype(BF16),
    )


def _layer(cfg, x, xn, mats, layer, prm, st_sc, st_mc, h_all, h_new, st_fc):
    d, d_ff = cfg.d, cfg.d_ff
    nrow = cfg.db * cfg.ds
    srows = slice(cfg.sb * cfg.tm, cfg.sb * cfg.tm + nrow)
    proj = _mm(cfg, xn, mats["w_in"], layer, cfg.nmain, BF16, "proj")
    dtr_g = _mm(cfg, xn, mats["w_dt"], layer, N_GROUPS * LANES, F32, "proj_dt")
    v, p_sc, s_sc = _sconv(cfg, proj, _flat_state(cfg, st_sc), prm["sconv_w"])
    out_a = _mm(cfg, v, mats["wa_out"], layer, d, BF16, "wa_out")
    yg, p_h, p_mc = _ssd_prompt(cfg, proj, dtr_g, prm)
    yg_s, h_new = _ssd_sample(cfg, proj, dtr_g, _flat_state(cfg, st_mc), h_all, layer, h_new, prm)
    yg = lax.dynamic_update_slice(yg, yg_s, (cfg.sb * cfg.tm, 0))
    s_mc = proj[srows, 7 * d:7 * d + cfg.conv_dim].astype(F32).reshape(cfg.db, cfg.ds, cfg.conv_dim)[:, cfg.ds - 3:]
    mix = _mm(cfg, yg, mats["wb_out"], layer, d, BF16, "wb_out", epilogue="mix",
              extras=((proj, 0, BF16), (proj, 1, BF16), (out_a, 0, BF16)))
    x = _mm(cfg, mix, mats["w_o"], layer, d, F32, "w_o", epilogue="resid", extras=((x, 0, F32),))
    xn2 = _rmsnorm(cfg, x, prm["norm2_g"], BF16)
    hmid, p_fc, ups = _ffn_up(cfg, xn2, mats["w_up"], layer, _flat_state(cfg, st_fc), prm["fconv_w"], prm["fconv_b"])
    s_fc = jnp.concatenate([ups[0, :nrow], ups[1, :nrow]], axis=-1).reshape(cfg.db, cfg.ds, 2 * d_ff)[:, cfg.ds - 2:]
    x = _mm_ksplit(cfg, hmid, mats["w_down"], layer, x, "w_down")
    return x, h_new, (p_sc, p_mc, p_h, p_fc, s_sc, s_mc, s_fc)


def kernel(x_prompt, x_sample, state_sconv, state_mconv, state_ssm, state_fconv, meta_tokens, norm1_g, w_in,
           sconv_w, wa_out, mconv_w, mconv_b, dt_bias, a_log, d_skip, gnorm_w, wb_out, w_o, norm2_g, w_up,
           fconv_w, fconv_b, w_down, final_g):
    cfg = _make_cfg(x_prompt, x_sample, state_ssm, w_up)
    depth = w_in.shape[0]
    dtype = x_prompt.dtype
    nrow = cfg.db * cfg.ds
    meta = jnp.broadcast_to(meta_tokens.astype(dtype)[None], (cfg.bp, N_META, cfg.d))
    x = jnp.concatenate([
        jnp.concatenate([meta, x_prompt], axis=1).reshape(cfg.bp * cfg.lp, cfg.d),
        x_sample.reshape(nrow, cfg.d),
        jnp.zeros((cfg.r - cfg.bp * cfg.lp - nrow, cfg.d), dtype)], axis=0).astype(F32)
    mats = _matrices(cfg, w_in, wa_out, wb_out, w_o, w_up, w_down)
    states = []
    h_new = None
    for l in range(depth):
        prm = _layer_params(cfg, sconv_w[l], mconv_w[l], mconv_b[l], dt_bias[l], a_log[l], d_skip[l], gnorm_w[l],
                            fconv_w[l], fconv_b[l])
        prm["norm2_g"] = norm2_g[l]
        xn = _rmsnorm(cfg, x, norm1_g[l], BF16)
        x, h_new, st = _layer(cfg, x, xn, mats, l, prm, state_sconv[l], state_mconv[l], state_ssm, h_new,
                              state_fconv[l])
        states.append(st)
    y = _rmsnorm(cfg, x, final_g, F32)
    y_prompt = y[:cfg.bp * cfg.lp].reshape(cfg.bp, cfg.lp, cfg.d)[:, N_META:].astype(dtype)
    y_sample = y[cfg.sb * cfg.tm:cfg.sb * cfg.tm + nrow].reshape(cfg.db, cfg.ds, cfg.d).astype(dtype)
    p_sc, p_mc, p_h, p_fc, s_sc, s_mc, s_fc = [jnp.stack([s[i] for s in states]).astype(dtype) for i in range(7)]
    return (y_prompt, y_sample, p_sc, p_mc, p_h, p_fc, s_sc, s_mc, h_new.astype(dtype), s_fc)
```

```python
import functools
import math
from typing import NamedTuple

import jax
import jax.numpy as jnp
from jax import lax
from jax.experimental import pallas as pl
from jax.experimental.pallas import tpu as pltpu

F32 = jnp.float32
BF16 = jnp.bfloat16

N_META = 16
N_GROUPS = 8
HEAD_DIM = 64
SSD_CHUNK = 128
EPS = 1e-6

LANES = 128
SUBLANES = 8
BF16_ROWS = 16
VMEM_PHYSICAL_BYTES = 64 * 1024 * 1024
VMEM_BUDGET_BYTES = 56 * 1024 * 1024
COMPILER_SCRATCH_BYTES = 6 * 1024 * 1024
NORM_ROWS = 256
MM_VMEM_BYTES = 48 * 1024 * 1024


class Cfg(NamedTuple):
    d: int
    bp: int
    seq: int
    lp: int
    db: int
    ds: int
    h: int
    hpg: int
    gw: int
    n: int
    d_inner: int
    conv_dim: int
    d_ff: int
    nt: int
    tm: int
    sb: int
    r: int
    nmain: int


def _largest_tile(dim, target, quantum=LANES):
    best = None
    t = quantum
    while t <= min(dim, target):
        if dim % t == 0:
            best = t
        t += quantum
    assert best is not None, (dim, target, quantum)
    return best


def _make_cfg(x_prompt, x_sample, state_ssm, w_up):
    bp, seq, d = x_prompt.shape
    db, ds, _ = x_sample.shape
    _, _, h, p, n = state_ssm.shape
    assert p == HEAD_DIM and h % N_GROUPS == 0 and seq % SSD_CHUNK == 0
    hpg = h // N_GROUPS
    assert hpg % 2 == 0 and hpg <= LANES
    d_inner = h * HEAD_DIM
    lp = N_META + seq
    d_ff = w_up.shape[-1] // 2
    nt = 3 if (lp % (3 * BF16_ROWS) == 0 and lp // 3 >= db * ds) else 1
    tm = lp // nt
    assert tm % BF16_ROWS == 0 and db * ds <= tm and ds >= 3
    sb = bp * nt
    return Cfg(d=d, bp=bp, seq=seq, lp=lp, db=db, ds=ds, h=h, hpg=hpg, gw=hpg * HEAD_DIM, n=n,
               d_inner=d_inner, conv_dim=d_inner + 2 * N_GROUPS * n, d_ff=d_ff, nt=nt, tm=tm,
               sb=sb, r=(sb + 1) * tm, nmain=5 * d + d_inner + d_inner + 2 * N_GROUPS * n)


def _cparams(semantics, block_bytes, temp_bytes=0):
    need = 2 * block_bytes + temp_bytes + COMPILER_SCRATCH_BYTES
    return pltpu.CompilerParams(dimension_semantics=semantics,
                                vmem_limit_bytes=int(min(max(need, 16 << 20), VMEM_BUDGET_BYTES)))


def _nbytes(shape, dtype):
    return math.prod(shape) * jnp.dtype(dtype).itemsize


def _silu(x):
    half = 0.5 * x
    return half + half * jnp.tanh(half)


def _softplus(x):
    return jnp.maximum(x, 0.0) + jnp.log1p(jnp.exp(-jnp.abs(x)))


def _shift_down(x, s):
    return x if s == 0 else pltpu.roll(x, s, axis=0)


def _shift_up(x, s):
    return x if s == 0 else pltpu.roll(x, x.shape[0] - s, axis=0)


def _rows(shape):
    return lax.broadcasted_iota(jnp.int32, shape, 0)


def _dot(a, b):
    return jnp.dot(a, b, preferred_element_type=F32)


def _dot_nt(a, b):
    return lax.dot_general(a, b, (((1,), (1,)), ((), ())), preferred_element_type=F32)


def _expand_heads(v, e):
    hi = v.astype(BF16)
    lo = (v - hi.astype(F32)).astype(BF16)
    return _dot(hi, e) + _dot(lo, e)


def _rmsnorm_kernel(x_ref, g_ref, o_ref):
    x = x_ref[...]
    y = x * lax.rsqrt(jnp.mean(x * x, axis=-1, keepdims=True) + EPS)
    o_ref[...] = (y * g_ref[...]).astype(o_ref.dtype)


def _rmsnorm(cfg, x, g, out_dtype):
    tr = _largest_tile(cfg.r, NORM_ROWS, BF16_ROWS)
    blk = _nbytes((tr, cfg.d), F32) + _nbytes((tr, cfg.d), out_dtype)
    return pl.pallas_call(
        _rmsnorm_kernel,
        grid=(cfg.r // tr,),
        in_specs=[pl.BlockSpec((tr, cfg.d), lambda i: (i, 0)),
                  pl.BlockSpec((1, cfg.d), lambda i: (0, 0))],
        out_specs=pl.BlockSpec((tr, cfg.d), lambda i: (i, 0)),
        out_shape=jax.ShapeDtypeStruct((cfg.r, cfg.d), out_dtype),
        compiler_params=_cparams(("parallel",), blk, _nbytes((tr, cfg.d), F32)),
        name="rmsnorm",
    )(x, g.reshape(1, cfg.d))


class Cast(NamedTuple):
    src: jax.Array
    layer: int

    def chunk_rows(self, nsteps):
        rows = self.src.shape[1]
        return next(r for r in range(BF16_ROWS, rows + 1, BF16_ROWS) if rows % r == 0 and rows // r <= nsteps)

    def specs(self, nsteps, step_of):
        _, rows, cols = self.src.shape
        r = self.chunk_rows(nsteps)
        last = rows // r - 1
        idx = lambda *g: jnp.minimum(step_of(*g), last)
        return (pl.BlockSpec((None, r, cols), lambda *g: (self.layer, idx(*g), 0)),
                pl.BlockSpec((r, cols), lambda *g: (idx(*g), 0)),
                jax.ShapeDtypeStruct((rows, cols), BF16),
                _nbytes((r, cols), F32) + _nbytes((r, cols), BF16))


def _run_casts(src_refs, dst_refs):
    for s, d in zip(src_refs, dst_refs):
        d[...] = s[...].astype(d.dtype)


def _mm_kernel(x_ref, w_ref, *refs, epilogue, n_extra, n_cast):
    o_ref = refs[n_extra + n_cast]
    _run_casts(refs[n_extra:n_extra + n_cast], refs[n_extra + n_cast + 1:])
    acc = _dot(x_ref[...], w_ref[...])
    if epilogue == "plain":
        o_ref[...] = acc.astype(o_ref.dtype)
    elif epilogue == "resid":
        o_ref[...] = refs[0][...] + acc
    elif epilogue == "mix":
        ga_ref, gb_ref, a_ref = refs[:3]
        ga = jax.nn.sigmoid(ga_ref[...].astype(F32))
        gb = jax.nn.sigmoid(gb_ref[...].astype(F32))
        o_ref[...] = (ga * a_ref[...].astype(F32) + gb * acc).astype(o_ref.dtype)
    else:
        raise ValueError(epilogue)


def _mm(cfg, x, w, ncols, out_dtype, name, epilogue="plain", extras=(), casts=()):
    k = x.shape[1]
    per_elem = jnp.dtype(out_dtype).itemsize + sum(jnp.dtype(dt).itemsize for _, _, dt in extras)
    ni = cfg.r // cfg.tm

    def plan(tn):
        nb = ncols // tn
        cast_specs = [c.specs(ni * nb, lambda i, j: i * nb + j) for c in casts]
        blocks = (_nbytes((cfg.tm, k), BF16) + _nbytes((k, tn), BF16) + cfg.tm * tn * per_elem
                  + sum(s[3] for s in cast_specs))
        return nb, cast_specs, blocks

    tn = next(t for t in (1024, 512, 256, LANES)
              if ncols % t == 0 and 2 * plan(t)[2] + 2 * _nbytes((cfg.tm, t), F32) <= MM_VMEM_BYTES)
    nb, cast_specs, blocks = plan(tn)
    tile = lambda off: pl.BlockSpec((cfg.tm, tn), lambda i, j: (i, j + off * nb))
    outs = pl.pallas_call(
        functools.partial(_mm_kernel, epilogue=epilogue, n_extra=len(extras), n_cast=len(casts)),
        grid=(ni, nb),
        in_specs=[pl.BlockSpec((cfg.tm, k), lambda i, j: (i, 0)),
                  pl.BlockSpec((k, tn), lambda i, j: (0, j))]
                 + [tile(off) for _, off, _ in extras] + [s[0] for s in cast_specs],
        out_specs=[tile(0)] + [s[1] for s in cast_specs],
        out_shape=[jax.ShapeDtypeStruct((cfg.r, ncols), out_dtype)] + [s[2] for s in cast_specs],
        compiler_params=_cparams(("arbitrary", "arbitrary"), blocks, 2 * _nbytes((cfg.tm, tn), F32)),
        name=name,
    )(x, w, *[a for a, _, _ in extras], *[c.src for c in casts])
    return outs if casts else outs[0]


def _prompt_taps(x, k):
    row = _rows(x.shape)
    return [x] + [jnp.where(row >= s, _shift_down(x, s), 0.0) for s in range(1, k)]


def _sample_taps(x, e, k, ds):
    t = _rows(x.shape) % ds
    return [x] + [jnp.where(t >= s, _shift_down(x, s), _shift_up(e, k - 1 - s)) for s in range(1, k)]


def _conv_from_taps(taps, w, b=None):
    k = len(taps)
    y = taps[0] * w[k - 1:k]
    for s in range(1, k):
        y = y + taps[s] * w[k - 1 - s:k - s]
    return y if b is None else y + b


def _sconv_prompt_kernel(h_ref, b_ref, c_ref, w_ref, v_ref, st_ref, *, lp):
    ch = c_ref[...].astype(F32) * h_ref[...].astype(F32)
    u = _conv_from_taps(_prompt_taps(ch, 3), w_ref[...])
    v_ref[...] = (b_ref[...].astype(F32) * u).astype(v_ref.dtype)
    st_ref[0] = ch[lp - SUBLANES:lp][SUBLANES - 2:]


def _sconv_sample_kernel(h_ref, b_ref, c_ref, e_ref, w_ref, v_ref, ch_ref, *, ds):
    ch = c_ref[...].astype(F32) * h_ref[...].astype(F32)
    u = _conv_from_taps(_sample_taps(ch, e_ref[...], 3, ds), w_ref[...])
    v_ref[...] = (b_ref[...].astype(F32) * u).astype(v_ref.dtype)
    ch_ref[...] = ch


def _sconv(cfg, proj, e_sc, w):
    d = cfg.d
    tc = _largest_tile(d, 512)
    nb = d // tc
    pblk = lambda off: pl.BlockSpec((cfg.lp, tc), lambda b, j: (b, j + off * nb))
    blk = 4 * _nbytes((cfg.lp, tc), BF16)
    v_p, st_p = pl.pallas_call(
        functools.partial(_sconv_prompt_kernel, lp=cfg.lp),
        grid=(cfg.bp, nb),
        in_specs=[pblk(2), pblk(3), pblk(4), pl.BlockSpec((3, tc), lambda b, j: (0, j))],
        out_specs=[pl.BlockSpec((cfg.lp, tc), lambda b, j: (b, j)),
                   pl.BlockSpec((1, 2, tc), lambda b, j: (b, 0, j))],
        out_shape=[jax.ShapeDtypeStruct((cfg.r, d), BF16),
                   jax.ShapeDtypeStruct((cfg.bp, 2, d), F32)],
        compiler_params=_cparams(("parallel", "parallel"), blk, 6 * _nbytes((cfg.lp, tc), F32)),
        name="sconv_prompt",
    )(proj, proj, proj, w)
    sblk = lambda off: pl.BlockSpec((cfg.tm, tc), lambda j: (cfg.sb, j + off * nb))
    own = pl.BlockSpec((cfg.tm, tc), lambda j: (0, j))
    blk = 4 * _nbytes((cfg.tm, tc), BF16) + 2 * _nbytes((cfg.tm, tc), F32)
    v_s, ch_s = pl.pallas_call(
        functools.partial(_sconv_sample_kernel, ds=cfg.ds),
        grid=(nb,),
        in_specs=[sblk(2), sblk(3), sblk(4), own, pl.BlockSpec((3, tc), lambda j: (0, j))],
        out_specs=[own, own],
        out_shape=[jax.ShapeDtypeStruct((cfg.tm, d), BF16),
                   jax.ShapeDtypeStruct((cfg.tm, d), F32)],
        compiler_params=_cparams(("parallel",), blk, 6 * _nbytes((cfg.tm, tc), F32)),
        name="sconv_sample",
    )(proj, proj, proj, e_sc, w)
    v = lax.dynamic_update_slice(v_p, v_s, (cfg.sb * cfg.tm, 0))
    st_s = ch_s[:cfg.db * cfg.ds].reshape(cfg.db, cfg.ds, d)[:, cfg.ds - 2:]
    return v, st_p, st_s


def _tile_taps(x, halo, k):
    row8 = _rows((SUBLANES, x.shape[1]))
    taps = [x]
    for s in range(1, k):
        body = _shift_down(x, s)
        top = jnp.where(row8 >= s, body[:SUBLANES], _shift_down(halo, s))
        taps.append(jnp.concatenate([top, body[SUBLANES:]], axis=0))
    return taps


def _ffn_up_kernel(x_ref, wa_ref, wg_ref, ea_ref, eg_ref, cwa_ref, cwg_ref, cba_ref, cbg_ref, *refs, cfg, n_cast):
    h_ref, pst_ref, ups_ref = refs[n_cast:n_cast + 3]
    raw_ref, halo_ref = refs[-2:]
    _run_casts(refs[:n_cast], refs[n_cast + 3:-2])
    i = pl.program_id(0)
    j = pl.program_id(1)
    tm = cfg.tm

    @pl.when(jnp.logical_and(i == 0, j == 0))
    def _():
        raw_ref[...] = jnp.zeros_like(raw_ref)
        halo_ref[...] = jnp.zeros_like(halo_ref)

    def variant(sample):
        x = x_ref[...]
        raw_ref[0, SUBLANES:, :] = _dot(x, wa_ref[...])
        raw_ref[1, SUBLANES:, :] = _dot(x, wg_ref[...])
        lag = lambda half, s: raw_ref[half, pl.ds(SUBLANES - s, tm), :]
        if sample:
            t = _rows((tm, 1)) % cfg.ds
            taps = [[lag(half, 0)] + [jnp.where(t >= s, lag(half, s), e_ref[pl.ds(2 - s, tm), :]) for s in (1, 2)]
                    for half, e_ref in ((0, ea_ref), (1, eg_ref))]
            ups_ref[0] = taps[0][0]
            ups_ref[1] = taps[1][0]
        else:
            inside = i % cfg.nt != 0
            for half in (0, 1):
                raw_ref[half, :SUBLANES, :] = jnp.where(inside, halo_ref[j, half], 0.0)
            taps = [[lag(half, s) for s in range(3)] for half in (0, 1)]
        act = _conv_from_taps(taps[0], cwa_ref[...], cba_ref[...])
        gate = _conv_from_taps(taps[1], cwg_ref[...], cbg_ref[...])
        h_ref[...] = (_silu(act) * gate).astype(h_ref.dtype)
        for half in (0, 1):
            tail = raw_ref[half, tm:, :]
            pst_ref[0, half] = tail[SUBLANES - 2:]
            halo_ref[j, half] = tail

    pl.when(i < cfg.sb)(functools.partial(variant, False))
    pl.when(i == cfg.sb)(functools.partial(variant, True))


def _ffn_up(cfg, xn, w_up, e_fc, cw, cb, casts=()):
    d_ff, tm, sb, nt = cfg.d_ff, cfg.tm, cfg.sb, cfg.nt
    k = xn.shape[1]
    tn = _largest_tile(d_ff, 512)
    nb = d_ff // tn
    ni = cfg.r // tm
    only_sb = lambda i, j: jnp.where(i == sb, j, 0)
    cast_specs = [c.specs(ni * nb, lambda i, j: i * nb + j) for c in casts]
    blk = (_nbytes((tm, k), BF16) + 2 * _nbytes((k, tn), BF16) + 2 * _nbytes((tm, tn), F32)
           + _nbytes((tm, tn), BF16) + 2 * _nbytes((tm, tn), F32) + sum(s[3] for s in cast_specs))
    h, pst, ups, *cast_out = pl.pallas_call(
        functools.partial(_ffn_up_kernel, cfg=cfg, n_cast=len(casts)),
        grid=(ni, nb),
        in_specs=[pl.BlockSpec((tm, k), lambda i, j: (i, 0)),
                  pl.BlockSpec((k, tn), lambda i, j: (0, j)),
                  pl.BlockSpec((k, tn), lambda i, j: (0, j + nb)),
                  pl.BlockSpec((tm + SUBLANES, tn), lambda i, j: (0, only_sb(i, j))),
                  pl.BlockSpec((tm + SUBLANES, tn), lambda i, j: (0, only_sb(i, j) + nb)),
                  pl.BlockSpec((3, tn), lambda i, j: (0, j)), pl.BlockSpec((3, tn), lambda i, j: (0, j + nb)),
                  pl.BlockSpec((1, tn), lambda i, j: (0, j)), pl.BlockSpec((1, tn), lambda i, j: (0, j + nb))]
                 + [s[0] for s in cast_specs],
        out_specs=[pl.BlockSpec((tm, tn), lambda i, j: (i, j)),
                   pl.BlockSpec((1, 2, 2, tn), lambda i, j: (i, 0, 0, j)),
                   pl.BlockSpec((2, tm, tn), lambda i, j: (0, 0, only_sb(i, j)))] + [s[1] for s in cast_specs],
        out_shape=[jax.ShapeDtypeStruct((cfg.r, d_ff), BF16),
                   jax.ShapeDtypeStruct((ni, 2, 2, d_ff), F32),
                   jax.ShapeDtypeStruct((2, tm, d_ff), F32)] + [s[2] for s in cast_specs],
        scratch_shapes=[pltpu.VMEM((2, SUBLANES + tm, tn), F32), pltpu.VMEM((nb, 2, SUBLANES, tn), F32)],
        compiler_params=_cparams(("arbitrary", "arbitrary"), blk, 16 * _nbytes((tm, tn), F32)),
        name="ffn_up",
    )(xn, w_up, w_up, e_fc, e_fc, cw, cw, cb, cb, *[c.src for c in casts])
    last = pst[nt - 1:sb:nt]
    p_fc = jnp.concatenate([last[:, 0], last[:, 1]], axis=-1)
    return h, p_fc, ups, cast_out


def _ssd_gate_norm(y, z, gnw):
    y = y * _silu(z)
    return y * lax.rsqrt(jnp.mean(y * y, axis=-1, keepdims=True) + EPS) * gnw


def _ssd_prompt_kernel(x_ref, bm_ref, cm_ref, z_ref, dtr_ref, wx_ref, wb_ref, wc_ref, bx_ref, bb_ref, bc_ref,
                       dtb_ref, alog_ref, dsk_ref, gnw_ref, e_ref, tril_ref,
                       y_ref, hfin_ref, mcx_ref, mcb_ref, mcc_ref,
                       ht_ref, hx_ref, hb_ref, hc_ref, *, cfg):
    lp, hpg, n, gw = cfg.lp, cfg.hpg, cfg.n, cfg.gw
    c = SSD_CHUNK
    nch = 1 + cfg.seq // c
    ht_ref[...] = jnp.zeros_like(ht_ref)
    hx_ref[...] = jnp.zeros_like(hx_ref)
    hb_ref[...] = jnp.zeros_like(hb_ref)
    hc_ref[...] = jnp.zeros_like(hc_ref)
    a_neg = -jnp.exp(alog_ref[0])
    dtb = dtb_ref[0]
    e = e_ref[...]
    tril = tril_ref[...]
    li = lax.broadcasted_iota(jnp.int32, (c, c), 0)
    si = lax.broadcasted_iota(jnp.int32, (c, c), 1)
    causal = li >= si
    lane_lo = lax.broadcasted_iota(jnp.int32, (c, 2 * HEAD_DIM), 1) < HEAD_DIM

    def conv_silu(src_ref, nxt, buf_ref, w_ref, b_ref, first):
        y = _silu(_conv_from_taps([buf_ref[pl.ds(SUBLANES - s, c), :] for s in range(4)], w_ref[...], b_ref[...]))
        buf_ref[:SUBLANES, :] = jnp.where(first, buf_ref[N_META:N_META + SUBLANES, :], buf_ref[c:, :])
        buf_ref[SUBLANES:, :] = src_ref[nxt, :].astype(F32)
        return y

    for src_ref, buf_ref in ((x_ref, hx_ref), (bm_ref, hb_ref), (cm_ref, hc_ref)):
        buf_ref[SUBLANES:, :] = src_ref[:c, :].astype(F32)

    def chunk(k, carry):
        first = k == 0
        start = pl.multiple_of(jnp.where(first, 0, N_META + c * (k - 1)), BF16_ROWS)
        rows = pl.ds(start, c)
        nxt = pl.ds(pl.multiple_of(jnp.minimum(N_META + c * k, lp - c), BF16_ROWS), c)
        xh = conv_silu(x_ref, nxt, hx_ref, wx_ref, bx_ref, first)
        bc = conv_silu(bm_ref, nxt, hb_ref, wb_ref, bb_ref, first)
        cc = conv_silu(cm_ref, nxt, hc_ref, wc_ref, bc_ref, first)
        live = jnp.logical_or(jnp.logical_not(first), _rows((c, LANES)) < N_META)
        dt = jnp.where(live, _softplus(dtr_ref[rows, :] + dtb), 0.0)
        a = dt * a_neg
        a1 = a.astype(BF16)
        r1 = a - a1.astype(F32)
        a2 = r1.astype(BF16)
        a3 = (r1 - a2.astype(F32)).astype(BF16)
        cs = _dot(tril, a1) + _dot(tril, a2) + _dot(tril, a3)
        cs_t = cs.T
        cs_last = cs[c - 1:c]
        ecs = jnp.exp(cs)
        both_e = _dot(jnp.concatenate([dt, dt * jnp.exp(cs_last - cs)], axis=0).astype(BF16), e)
        xs_b = (xh * both_e[:c]).astype(BF16)
        xsw_b = (xh * both_e[c:]).astype(BF16)
        dec_row = _expand_heads(jnp.broadcast_to(ecs[c - 1:c], (SUBLANES, LANES)), e)[:1]
        scores = jnp.where(causal, _dot_nt(cc.astype(BF16), bc.astype(BF16)), 0.0)
        ht = ht_ref[...]
        ht_b = ht.astype(BF16)
        pairs = []
        for q in range(hpg // 2):
            cols = slice(2 * HEAD_DIM * q, 2 * HEAD_DIM * (q + 1))
            rhs = jnp.concatenate([xs_b[:, cols], ht_b[:, cols]], axis=0)
            res = []
            for r in (2 * q, 2 * q + 1):
                m = scores * jnp.exp(jnp.minimum(cs[:, r:r + 1] - cs_t[r:r + 1, :], 0.0))
                lhs = jnp.concatenate([m, cc * ecs[:, r:r + 1]], axis=1).astype(BF16)
                res.append(_dot(lhs, rhs))
            pairs.append(jnp.where(lane_lo, res[0], res[1]))
        y = jnp.concatenate(pairs, axis=1) if len(pairs) > 1 else pairs[0]
        ht_ref[...] = ht * dec_row + _dot(bc.T.astype(BF16), xsw_b)
        y = y + dsk_ref[...] * xh
        y_ref[rows, :] = _ssd_gate_norm(y, z_ref[rows, :].astype(F32), gnw_ref[...]).astype(y_ref.dtype)
        return carry

    lax.fori_loop(0, nch, chunk, 0)
    hfin_ref[0] = ht_ref[...].T.reshape(hpg, HEAD_DIM, n)
    tail = lambda ref: ref[lp - BF16_ROWS:lp, :].astype(F32)[BF16_ROWS - 3:]
    mcx_ref[0] = tail(x_ref)
    mcb_ref[0] = tail(bm_ref)
    mcc_ref[0] = tail(cm_ref)


def _ssd_prompt(cfg, proj, dtr_g, prm):
    lp, gw, n, hpg, g_ = cfg.lp, cfg.gw, cfg.n, cfg.hpg, N_GROUPS
    xb0 = 7 * cfg.d // gw
    zb0 = 5 * cfg.d // gw
    bb0 = 9 * cfg.d // n
    cvb = cfg.d_inner // n
    colblk = lambda w, f: pl.BlockSpec((lp, w), f)
    par = lambda rows, w, f: pl.BlockSpec((rows, w), f)
    vec3 = pl.BlockSpec((1, 1, LANES), lambda b, g: (g, 0, 0))
    in_specs = [
        colblk(gw, lambda b, g: (b, xb0 + g)), colblk(n, lambda b, g: (b, bb0 + g)),
        colblk(n, lambda b, g: (b, bb0 + g_ + g)), colblk(gw, lambda b, g: (b, zb0 + g)),
        colblk(LANES, lambda b, g: (b, g)),
        par(4, gw, lambda b, g: (0, g)), par(4, n, lambda b, g: (0, cvb + g)), par(4, n, lambda b, g: (0, cvb + g_ + g)),
        par(1, gw, lambda b, g: (0, g)), par(1, n, lambda b, g: (0, cvb + g)), par(1, n, lambda b, g: (0, cvb + g_ + g)),
        vec3, vec3,
        par(1, gw, lambda b, g: (0, g)), par(1, gw, lambda b, g: (0, g)),
        par(LANES, gw, lambda b, g: (0, 0)), par(SSD_CHUNK, SSD_CHUNK, lambda b, g: (0, 0)),
    ]
    out_specs = [
        pl.BlockSpec((lp, gw), lambda b, g: (b, g)),
        pl.BlockSpec((1, hpg, HEAD_DIM, n), lambda b, g: (b, g, 0, 0)),
        pl.BlockSpec((1, 3, gw), lambda b, g: (b, 0, g)),
        pl.BlockSpec((1, 3, n), lambda b, g: (b, 0, g)),
        pl.BlockSpec((1, 3, n), lambda b, g: (b, 0, g)),
    ]
    out_shape = [
        jax.ShapeDtypeStruct((cfg.r, cfg.d_inner), BF16),
        jax.ShapeDtypeStruct((cfg.bp, cfg.h, HEAD_DIM, n), F32),
        jax.ShapeDtypeStruct((cfg.bp, 3, cfg.d_inner), F32),
        jax.ShapeDtypeStruct((cfg.bp, 3, g_ * n), F32),
        jax.ShapeDtypeStruct((cfg.bp, 3, g_ * n), F32),
    ]
    blk = 3 * _nbytes((lp, gw), BF16) + 2 * _nbytes((lp, n), BF16) + _nbytes((lp, LANES), F32)
    y, hfin, mcx, mcb, mcc = pl.pallas_call(
        functools.partial(_ssd_prompt_kernel, cfg=cfg),
        grid=(cfg.bp, g_),
        in_specs=in_specs, out_specs=out_specs, out_shape=out_shape,
        scratch_shapes=[pltpu.VMEM((n, gw), F32), pltpu.VMEM((SUBLANES + SSD_CHUNK, gw), F32),
                        pltpu.VMEM((SUBLANES + SSD_CHUNK, n), F32), pltpu.VMEM((SUBLANES + SSD_CHUNK, n), F32)],
        compiler_params=_cparams(("parallel", "parallel"), blk, 24 * _nbytes((SSD_CHUNK, gw), F32) + (4 << 20)),
        name="ssd_prompt",
    )(proj, proj, proj, proj, dtr_g,
      prm["mconv_w"], prm["mconv_w"], prm["mconv_w"], prm["mconv_b"], prm["mconv_b"], prm["mconv_b"],
      prm["dt_bias_g"], prm["a_log_g"], prm["d_skip_e"], prm["gnorm_w"], prm["expand"], prm["tril"])
    return y, hfin, jnp.concatenate([mcx, mcb, mcc], axis=-1)


def _ssd_tok_kernel(x_ref, bm_ref, cm_ref, dtr_ref, ex_ref, eb_ref, ec_ref, wx_ref, wb_ref, wc_ref,
                    bx_ref, bb_ref, bc_ref, dtb_ref, alog_ref, dsk_ref, e_ref,
                    ypre_ref, ecs_ref, xsw_ref, bo_ref, co_ref, dec_ref, *, ds):
    e = e_ref[...]
    conv = lambda ref, eref, w, b: _silu(_conv_from_taps(
        _sample_taps(ref[...].astype(F32), eref[...], 4, ds), w[...], b[...]))
    xh = conv(x_ref, ex_ref, wx_ref, bx_ref)
    bc = conv(bm_ref, eb_ref, wb_ref, bb_ref)
    cc = conv(cm_ref, ec_ref, wc_ref, bc_ref)
    dt = _softplus(dtr_ref[...] + dtb_ref[0])
    a = dt * (-jnp.exp(alog_ref[0]))
    t = _rows(a.shape) % ds
    t1 = _rows((a.shape[0], 1)) % ds
    cs = a
    for j in range(1, ds):
        cs = cs + jnp.where(t >= j, _shift_down(a, j), 0.0)
    cs_last = jnp.zeros_like(cs)
    for j in range(ds):
        cs_last = cs_last + jnp.where(t == ds - 1 - j, _shift_up(cs, j), 0.0)
    ecs_e = _expand_heads(jnp.exp(cs), e)
    xs = xh * _expand_heads(dt, e)
    y = dsk_ref[...] * xh
    for j in range(ds):
        sc = jnp.sum(cc * _shift_down(bc, j), axis=-1, keepdims=True)
        diff = jnp.where(t >= j, cs - _shift_down(cs, j), 0.0)
        coef = jnp.where(t1 >= j, sc, 0.0) * jnp.exp(diff)
        y = y + _expand_heads(coef, e) * _shift_down(xs, j)
    ypre_ref[...] = y
    ecs_ref[...] = ecs_e
    xsw_ref[...] = xs * _expand_heads(jnp.exp(cs_last - cs), e)
    bo_ref[...] = bc
    co_ref[...] = cc
    dec_ref[...] = jnp.exp(cs_last)


def _ssd_state_kernel(dec_ref, h0_ref, c_ref, b_ref, xsw_ref, *refs, cfg):
    hout_ref, yoff_ref = refs[-2:]
    hpg, n, gw, ds = cfg.hpg, cfg.n, cfg.gw, cfg.ds
    i = pl.program_id(0)
    nseq = SUBLANES // ds
    pad = jnp.zeros((LANES - SUBLANES, n), F32)
    row8 = _rows((SUBLANES, 1)) // ds
    cmat = c_ref[...]
    bmat = b_ref[...]
    xsw = xsw_ref[...]
    youts = []
    for g in range(N_GROUPS):
        cg = cmat[:, g * n:(g + 1) * n].astype(BF16)
        bg = bmat[:, g * n:(g + 1) * n]
        xw = jnp.concatenate([xsw[:, g * gw:(g + 1) * gw], jnp.zeros((LANES - SUBLANES, gw), F32)], axis=0)
        xw_t = xw.T.astype(BF16)
        yg = jnp.zeros((SUBLANES, gw), F32)
        for s in range(nseq):
            h0 = h0_ref[s, g * hpg:(g + 1) * hpg].reshape(gw, n)
            yg = jnp.where(row8 == s, _dot_nt(cg, h0.astype(BF16)), yg)
            bs = jnp.concatenate([jnp.where(row8 == s, bg, 0.0), pad], axis=0).astype(BF16)
            dh = _dot(xw_t, bs)
            for r in range(hpg):
                sl = slice(r * HEAD_DIM, (r + 1) * HEAD_DIM)
                hout_ref[s, g * hpg + r] = h0[sl] * dec_ref[i * nseq + s, g * hpg + r] + dh[sl]
        youts.append(yg)
    yoff_ref[...] = jnp.concatenate(youts, axis=1)


def _ssd_post_kernel(ypre_ref, yoff_ref, ecs_ref, z_ref, gnw_ref, o_ref):
    y = ypre_ref[...] + yoff_ref[...] * ecs_ref[...]
    o_ref[...] = _ssd_gate_norm(y, z_ref[...].astype(F32), gnw_ref[...]).astype(o_ref.dtype)


def _ssd_sample(cfg, proj, dtr_g, e_mc, h_all, layer, h_new, prm):
    tm, gw, n, g_, sb = cfg.tm, cfg.gw, cfg.n, N_GROUPS, cfg.sb
    xb0 = 7 * cfg.d // gw
    zb0 = 5 * cfg.d // gw
    bb0 = 9 * cfg.d // n
    cvb = cfg.d_inner // n
    row = lambda w, f: pl.BlockSpec((tm, w), f)
    par = lambda rows, w, f: pl.BlockSpec((rows, w), f)
    vec3 = pl.BlockSpec((1, 1, LANES), lambda g: (g, 0, 0))
    in_specs = [
        row(gw, lambda g: (sb, xb0 + g)), row(n, lambda g: (sb, bb0 + g)), row(n, lambda g: (sb, bb0 + g_ + g)),
        row(LANES, lambda g: (sb, g)),
        row(gw, lambda g: (0, g)), row(n, lambda g: (0, cvb + g)), row(n, lambda g: (0, cvb + g_ + g)),
        par(4, gw, lambda g: (0, g)), par(4, n, lambda g: (0, cvb + g)), par(4, n, lambda g: (0, cvb + g_ + g)),
        par(1, gw, lambda g: (0, g)), par(1, n, lambda g: (0, cvb + g)), par(1, n, lambda g: (0, cvb + g_ + g)),
        vec3, vec3, par(1, gw, lambda g: (0, g)), par(LANES, gw, lambda g: (0, 0)),
    ]
    own = lambda w: pl.BlockSpec((tm, w), lambda g: (0, g))
    blk = 2 * _nbytes((tm, gw), BF16) + 5 * _nbytes((tm, gw), F32)
    ypre, ecs, xsw, bo, co, dec = pl.pallas_call(
        functools.partial(_ssd_tok_kernel, ds=cfg.ds),
        grid=(g_,),
        in_specs=in_specs,
        out_specs=[own(gw), own(gw), own(gw), own(n), own(n), own(LANES)],
        out_shape=[jax.ShapeDtypeStruct((tm, cfg.d_inner), F32)] * 3
                  + [jax.ShapeDtypeStruct((tm, g_ * n), F32)] * 2
                  + [jax.ShapeDtypeStruct((tm, g_ * LANES), F32)],
        compiler_params=_cparams(("parallel",), blk, 16 * _nbytes((tm, gw), F32)),
        name="ssd_tok",
    )(proj, proj, proj, dtr_g, e_mc, e_mc, e_mc,
      prm["mconv_w"], prm["mconv_w"], prm["mconv_w"], prm["mconv_b"], prm["mconv_b"], prm["mconv_b"],
      prm["dt_bias_g"], prm["a_log_g"], prm["d_skip_e"], prm["expand"])
    nrow = cfg.db * cfg.ds
    dec_sh = dec[:nrow].reshape(cfg.db, cfg.ds, g_, LANES)[:, cfg.ds - 1, :, :cfg.hpg].reshape(cfg.db, cfg.h)
    nseq = SUBLANES // cfg.ds
    rows8 = lambda w: pl.BlockSpec((SUBLANES, w), lambda i: (i, 0))
    hblk = pl.BlockSpec((None, nseq, cfg.h, HEAD_DIM, n), lambda i: (layer, i, 0, 0, 0))
    blk = 2 * _nbytes((nseq, cfg.h, HEAD_DIM, n), F32) + 3 * _nbytes((SUBLANES, cfg.d_inner), F32)
    hbm = pl.BlockSpec(memory_space=pl.ANY)
    aliased = [jnp.zeros((tm, cfg.d_inner), F32)] + ([] if h_new is None else [h_new])
    hout, yoff = pl.pallas_call(
        functools.partial(_ssd_state_kernel, cfg=cfg),
        grid=(cfg.db // nseq,),
        in_specs=[pl.BlockSpec(memory_space=pltpu.SMEM), hblk, rows8(g_ * n), rows8(g_ * n), rows8(cfg.d_inner)]
                 + [hbm] * len(aliased),
        out_specs=[hblk, rows8(cfg.d_inner)],
        out_shape=[jax.ShapeDtypeStruct(h_all.shape, F32), jax.ShapeDtypeStruct((tm, cfg.d_inner), F32)],
        input_output_aliases={5: 1} if h_new is None else {5: 1, 6: 0},
        compiler_params=_cparams(("arbitrary",), blk, 8 << 20),
        name="ssd_state",
    )(dec_sh, h_all, co, bo, xsw, *aliased)
    blk = 3 * _nbytes((tm, gw), F32) + 2 * _nbytes((tm, gw), BF16)
    yg = pl.pallas_call(
        _ssd_post_kernel,
        grid=(g_,),
        in_specs=[own(gw), own(gw), own(gw), row(gw, lambda g: (sb, zb0 + g)), par(1, gw, lambda g: (0, g))],
        out_specs=own(gw),
        out_shape=jax.ShapeDtypeStruct((tm, cfg.d_inner), BF16),
        compiler_params=_cparams(("parallel",), blk, 4 * _nbytes((tm, gw), F32)),
        name="ssd_post",
    )(ypre, yoff, ecs, proj, prm["gnorm_w"])
    return yg, hout


def _flat_state(cfg, buf):
    db, km1, c = buf.shape
    e = jnp.pad(buf.astype(F32), ((0, 0), (0, cfg.ds - km1), (0, 0))).reshape(db * cfg.ds, c)
    return jnp.pad(e, ((0, cfg.tm + SUBLANES - db * cfg.ds), (0, 0)))


def _dt_weights(cfg, w_in):
    depth, g_, hpg = w_in.shape[0], N_GROUPS, cfg.hpg
    w_dt = w_in[:, :, cfg.nmain:].reshape(depth, cfg.d, g_, hpg)
    w_dt = jnp.pad(w_dt, ((0, 0), (0, 0), (0, 0), (0, LANES - hpg))).reshape(depth, cfg.d, g_ * LANES)
    return w_dt.astype(BF16)


def _layer_params(cfg, sconv_w, mconv_w, mconv_b, dt_bias, a_log, d_skip, gnorm_w, fconv_w, fconv_b):
    g_, hpg = N_GROUPS, cfg.hpg
    grp = lambda v: jnp.pad(v.astype(F32).reshape(g_, 1, hpg), ((0, 0), (0, 0), (0, LANES - hpg)))
    head = jnp.arange(LANES)[:, None]
    col = jnp.arange(cfg.gw)[None, :] // HEAD_DIM
    return dict(
        sconv_w=sconv_w.astype(F32),
        mconv_w=mconv_w.astype(F32), mconv_b=mconv_b.astype(F32).reshape(1, -1),
        dt_bias_g=grp(dt_bias), a_log_g=grp(a_log),
        d_skip_e=jnp.repeat(d_skip.astype(F32), HEAD_DIM).reshape(1, -1),
        gnorm_w=gnorm_w.astype(F32).reshape(1, -1),
        fconv_w=fconv_w.astype(F32), fconv_b=fconv_b.astype(F32).reshape(1, -1),
        expand=((head == col) & (head < hpg)).astype(BF16),
        tril=(jnp.arange(SSD_CHUNK)[:, None] >= jnp.arange(SSD_CHUNK)[None, :]).astype(BF16),
    )


def _layer(cfg, x, xn, w_in_b, w_dt_b, f32_mats, layer, prm, st_sc, st_mc, h_all, h_new, st_fc):
    d, d_ff = cfg.d, cfg.d_ff
    nrow = cfg.db * cfg.ds
    srows = slice(cfg.sb * cfg.tm, cfg.sb * cfg.tm + nrow)
    own = ("w_up", "w_down", "wa_out", "wb_out", "w_o")
    proj, *cast = _mm(cfg, xn, w_in_b, cfg.nmain, BF16, "proj", casts=[Cast(f32_mats[n], layer) for n in own])
    mats = dict(zip(own, cast))
    dtr_g = _mm(cfg, xn, w_dt_b, N_GROUPS * LANES, F32, "proj_dt")
    v, p_sc, s_sc = _sconv(cfg, proj, _flat_state(cfg, st_sc), prm["sconv_w"])
    out_a = _mm(cfg, v, mats["wa_out"], d, BF16, "wa_out")
    yg, p_h, p_mc = _ssd_prompt(cfg, proj, dtr_g, prm)
    yg_s, h_new = _ssd_sample(cfg, proj, dtr_g, _flat_state(cfg, st_mc), h_all, layer, h_new, prm)
    yg = lax.dynamic_update_slice(yg, yg_s, (cfg.sb * cfg.tm, 0))
    s_mc = proj[srows, 7 * d:7 * d + cfg.conv_dim].astype(F32).reshape(cfg.db, cfg.ds, cfg.conv_dim)[:, cfg.ds - 3:]
    mix = _mm(cfg, yg, mats["wb_out"], d, BF16, "wb_out", epilogue="mix",
              extras=((proj, 0, BF16), (proj, 1, BF16), (out_a, 0, BF16)))
    x = _mm(cfg, mix, mats["w_o"], d, F32, "w_o", epilogue="resid", extras=((x, 0, F32),))
    xn2 = _rmsnorm(cfg, x, prm["norm2_g"], BF16)
    more = layer + 1 < f32_mats["w_in"].shape[0]
    hmid, p_fc, ups, nxt = _ffn_up(cfg, xn2, mats["w_up"], _flat_state(cfg, st_fc), prm["fconv_w"], prm["fconv_b"],
                                   casts=[Cast(f32_mats["w_in"], layer + 1)] if more else [])
    s_fc = jnp.concatenate([ups[0, :nrow], ups[1, :nrow]], axis=-1).reshape(cfg.db, cfg.ds, 2 * d_ff)[:, cfg.ds - 2:]
    x = _mm(cfg, hmid, mats["w_down"], d, F32, "w_down", epilogue="resid", extras=((x, 0, F32),))
    return x, h_new, (p_sc, p_mc, p_h, p_fc, s_sc, s_mc, s_fc), (nxt[0] if more else None)


def kernel(x_prompt, x_sample, state_sconv, state_mconv, state_ssm, state_fconv, meta_tokens, norm1_g, w_in,
           sconv_w, wa_out, mconv_w, mconv_b, dt_bias, a_log, d_skip, gnorm_w, wb_out, w_o, norm2_g, w_up,
           fconv_w, fconv_b, w_down, final_g):
    cfg = _make_cfg(x_prompt, x_sample, state_ssm, w_up)
    depth = w_in.shape[0]
    dtype = x_prompt.dtype
    nrow = cfg.db * cfg.ds
    meta = jnp.broadcast_to(meta_tokens.astype(dtype)[None], (cfg.bp, N_META, cfg.d))
    x = jnp.concatenate([
        jnp.concatenate([meta, x_prompt], axis=1).reshape(cfg.bp * cfg.lp, cfg.d),
        x_sample.reshape(nrow, cfg.d),
        jnp.zeros((cfg.r - cfg.bp * cfg.lp - nrow, cfg.d), dtype)], axis=0).astype(F32)
    f32_mats = dict(w_in=w_in, wa_out=wa_out, wb_out=wb_out, w_o=w_o, w_up=w_up, w_down=w_down)
    w_dt_b = _dt_weights(cfg, w_in)
    w_in_b = w_in[0].astype(BF16)
    states = []
    h_new = None
    for l in range(depth):
        prm = _layer_params(cfg, sconv_w[l], mconv_w[l], mconv_b[l], dt_bias[l], a_log[l], d_skip[l], gnorm_w[l],
                            fconv_w[l], fconv_b[l])
        prm["norm2_g"] = norm2_g[l]
        xn = _rmsnorm(cfg, x, norm1_g[l], BF16)
        x, h_new, st, w_in_b = _layer(cfg, x, xn, w_in_b, w_dt_b[l], f32_mats, l, prm, state_sconv[l],
                                      state_mconv[l], state_ssm, h_new, state_fconv[l])
        states.append(st)
    y = _rmsnorm(cfg, x, final_g, F32)
    y_prompt = y[:cfg.bp * cfg.lp].reshape(cfg.bp, cfg.lp, cfg.d)[:, N_META:].astype(dtype)
    y_sample = y[cfg.sb * cfg.tm:cfg.sb * cfg.tm + nrow].reshape(cfg.db, cfg.ds, cfg.d).astype(dtype)
    p_sc, p_mc, p_h, p_fc, s_sc, s_mc, s_fc = [jnp.stack([s[i] for s in states]).astype(dtype) for i in range(7)]
    return (y_prompt, y_sample, p_sc, p_mc, p_h, p_fc, s_sc, s_mc, h_new.astype(dtype), s_fc)
```

```python
import functools
import math
from typing import NamedTuple

import jax
import jax.numpy as jnp
from jax import lax
from jax.experimental import pallas as pl
from jax.experimental.pallas import tpu as pltpu

F32 = jnp.float32
BF16 = jnp.bfloat16

N_META = 16
N_GROUPS = 8
HEAD_DIM = 64
SSD_CHUNK = 128
EPS = 1e-6

LANES = 128
SUBLANES = 8
BF16_ROWS = 16
VMEM_PHYSICAL_BYTES = 64 * 1024 * 1024
VMEM_BUDGET_BYTES = 56 * 1024 * 1024
COMPILER_SCRATCH_BYTES = 6 * 1024 * 1024
NORM_ROWS = 256
CAST_CHUNK_BYTES = 8 * 1024 * 1024
MM_VMEM_BYTES = 48 * 1024 * 1024


class Cfg(NamedTuple):
    d: int
    bp: int
    seq: int
    lp: int
    db: int
    ds: int
    h: int
    hpg: int
    gw: int
    n: int
    d_inner: int
    conv_dim: int
    d_ff: int
    nt: int
    tm: int
    sb: int
    r: int
    nmain: int


def _largest_tile(dim, target, quantum=LANES):
    best = None
    t = quantum
    while t <= min(dim, target):
        if dim % t == 0:
            best = t
        t += quantum
    assert best is not None, (dim, target, quantum)
    return best


def _make_cfg(x_prompt, x_sample, state_ssm, w_up):
    bp, seq, d = x_prompt.shape
    db, ds, _ = x_sample.shape
    _, _, h, p, n = state_ssm.shape
    assert p == HEAD_DIM and h % N_GROUPS == 0 and seq % SSD_CHUNK == 0
    hpg = h // N_GROUPS
    assert hpg % 2 == 0 and hpg <= LANES
    d_inner = h * HEAD_DIM
    lp = N_META + seq
    d_ff = w_up.shape[-1] // 2
    nt = 3 if (lp % (3 * BF16_ROWS) == 0 and lp // 3 >= db * ds) else 1
    tm = lp // nt
    assert tm % BF16_ROWS == 0 and db * ds <= tm and ds >= 3
    sb = bp * nt
    return Cfg(d=d, bp=bp, seq=seq, lp=lp, db=db, ds=ds, h=h, hpg=hpg, gw=hpg * HEAD_DIM, n=n,
               d_inner=d_inner, conv_dim=d_inner + 2 * N_GROUPS * n, d_ff=d_ff, nt=nt, tm=tm,
               sb=sb, r=(sb + 1) * tm, nmain=5 * d + d_inner + d_inner + 2 * N_GROUPS * n)


def _cparams(semantics, block_bytes, temp_bytes=0):
    need = 2 * block_bytes + temp_bytes + COMPILER_SCRATCH_BYTES
    return pltpu.CompilerParams(dimension_semantics=semantics,
                                vmem_limit_bytes=int(min(max(need, 16 << 20), VMEM_BUDGET_BYTES)))


def _nbytes(shape, dtype):
    return math.prod(shape) * jnp.dtype(dtype).itemsize


def _silu(x):
    half = 0.5 * x
    return half + half * jnp.tanh(half)


def _softplus(x):
    return jnp.maximum(x, 0.0) + jnp.log1p(jnp.exp(-jnp.abs(x)))


def _shift_down(x, s):
    return x if s == 0 else pltpu.roll(x, s, axis=0)


def _shift_up(x, s):
    return x if s == 0 else pltpu.roll(x, x.shape[0] - s, axis=0)


def _rows(shape):
    return lax.broadcasted_iota(jnp.int32, shape, 0)


def _dot(a, b):
    return jnp.dot(a, b, preferred_element_type=F32)


def _dot_nt(a, b):
    return lax.dot_general(a, b, (((1,), (1,)), ((), ())), preferred_element_type=F32)


def _expand_heads(v, e):
    hi = v.astype(BF16)
    lo = (v - hi.astype(F32)).astype(BF16)
    return _dot(hi, e) + _dot(lo, e)


def _rmsnorm_kernel(x_ref, g_ref, o_ref):
    x = x_ref[...]
    y = x * lax.rsqrt(jnp.mean(x * x, axis=-1, keepdims=True) + EPS)
    o_ref[...] = (y * g_ref[...]).astype(o_ref.dtype)


def _rmsnorm(cfg, x, g, out_dtype, row0=0, nrows=None):
    nrows = cfg.r if nrows is None else nrows
    quantum = BF16_ROWS if jnp.dtype(out_dtype).itemsize < 4 else SUBLANES
    tr = _largest_tile(math.gcd(nrows, row0) if row0 else nrows, NORM_ROWS if row0 == 0 else 2 * NORM_ROWS, quantum)
    blk = _nbytes((tr, cfg.d), F32) + _nbytes((tr, cfg.d), out_dtype)
    b0 = row0 // tr
    return pl.pallas_call(
        _rmsnorm_kernel,
        grid=(nrows // tr,),
        in_specs=[pl.BlockSpec((tr, cfg.d), lambda i: (i + b0, 0)),
                  pl.BlockSpec((1, cfg.d), lambda i: (0, 0))],
        out_specs=pl.BlockSpec((tr, cfg.d), lambda i: (i, 0)),
        out_shape=jax.ShapeDtypeStruct((nrows, cfg.d), out_dtype),
        compiler_params=_cparams(("parallel",), blk, _nbytes((tr, cfg.d), F32)),
        name="rmsnorm",
    )(x, g.reshape(1, cfg.d))


def _cast_kernel(s_ref, d_ref):
    d_ref[...] = s_ref[...].astype(d_ref.dtype)


def _cast_layer(w, layer):
    _, rows, cols = w.shape
    r = _largest_tile(rows, max(BF16_ROWS, CAST_CHUNK_BYTES // (cols * 6)), BF16_ROWS)
    return pl.pallas_call(
        _cast_kernel,
        grid=(rows // r,),
        in_specs=[pl.BlockSpec((None, r, cols), lambda i: (layer, i, 0))],
        out_specs=pl.BlockSpec((r, cols), lambda i: (i, 0)),
        out_shape=jax.ShapeDtypeStruct((rows, cols), BF16),
        compiler_params=_cparams(("parallel",), _nbytes((r, cols), F32) + _nbytes((r, cols), BF16)),
        name="cast_bf16",
    )(w)


class Cast(NamedTuple):
    src: jax.Array
    layer: int

    def chunk_rows(self, nsteps):
        rows = self.src.shape[1]
        return next(r for r in range(BF16_ROWS, rows + 1, BF16_ROWS) if rows % r == 0 and rows // r <= nsteps)

    def specs(self, nsteps, step_of):
        _, rows, cols = self.src.shape
        r = self.chunk_rows(nsteps)
        last = rows // r - 1
        idx = lambda *g: jnp.minimum(step_of(*g), last)
        return (pl.BlockSpec((None, r, cols), lambda *g: (self.layer, idx(*g), 0)),
                pl.BlockSpec((r, cols), lambda *g: (idx(*g), 0)),
                jax.ShapeDtypeStruct((rows, cols), BF16),
                _nbytes((r, cols), F32) + _nbytes((r, cols), BF16))


def _run_casts(src_refs, dst_refs):
    for s, d in zip(src_refs, dst_refs):
        d[...] = s[...].astype(d.dtype)


def _mm_kernel(x_ref, w_ref, *refs, epilogue, n_extra, n_cast, has_tail):
    if has_tail:
        xt_ref, refs = refs[0], refs[1:]
    o_ref = refs[n_extra + n_cast]
    _run_casts(refs[n_extra:n_extra + n_cast], refs[n_extra + n_cast + 1:])

    def product(lhs_ref):
        acc = _dot(lhs_ref[...], w_ref[...])
        if epilogue == "plain":
            o_ref[...] = acc.astype(o_ref.dtype)
        elif epilogue == "resid":
            o_ref[...] = refs[0][...] + acc
        elif epilogue == "mix":
            ga_ref, gb_ref, a_ref = refs[:3]
            ga = jax.nn.sigmoid(ga_ref[...].astype(F32))
            gb = jax.nn.sigmoid(gb_ref[...].astype(F32))
            o_ref[...] = (ga * a_ref[...].astype(F32) + gb * acc).astype(o_ref.dtype)
        else:
            raise ValueError(epilogue)

    if has_tail:
        last = pl.num_programs(0) - 1
        pl.when(pl.program_id(0) < last)(functools.partial(product, x_ref))
        pl.when(pl.program_id(0) == last)(functools.partial(product, xt_ref))
    else:
        product(x_ref)


def _mm(cfg, x, w, ncols, out_dtype, name, epilogue="plain", extras=(), casts=(), x_tail=None):
    k = x.shape[1]
    per_elem = jnp.dtype(out_dtype).itemsize + sum(jnp.dtype(dt).itemsize for _, _, dt in extras)
    ni = cfg.r // cfg.tm
    tails = [] if x_tail is None else [x_tail]
    xrow = (lambda i: i) if x_tail is None else (lambda i: jnp.minimum(i, ni - 2))

    def plan(tn):
        nb = ncols // tn
        cast_specs = [c.specs(ni * nb, lambda i, j: i * nb + j) for c in casts]
        blocks = (_nbytes((cfg.tm, k), BF16) + _nbytes((k, tn), BF16) + cfg.tm * tn * per_elem
                  + sum(s[3] for s in cast_specs))
        return nb, cast_specs, blocks

    single = len(tails) * _nbytes((cfg.tm, k), BF16)
    tn = next(t for t in (1024, 512, 256, LANES)
              if ncols % t == 0 and 2 * plan(t)[2] + single + 2 * _nbytes((cfg.tm, t), F32) <= MM_VMEM_BYTES)
    nb, cast_specs, blocks = plan(tn)
    tile = lambda off: pl.BlockSpec((cfg.tm, tn), lambda i, j: (i, j + off * nb))
    outs = pl.pallas_call(
        functools.partial(_mm_kernel, epilogue=epilogue, n_extra=len(extras), n_cast=len(casts),
                          has_tail=bool(tails)),
        grid=(ni, nb),
        in_specs=[pl.BlockSpec((cfg.tm, k), lambda i, j: (xrow(i), 0)),
                  pl.BlockSpec((k, tn), lambda i, j: (0, j))]
                 + [pl.BlockSpec((cfg.tm, k), lambda i, j: (0, 0), pipeline_mode=pl.Buffered(1)) for _ in tails]
                 + [tile(off) for _, off, _ in extras] + [s[0] for s in cast_specs],
        out_specs=[tile(0)] + [s[1] for s in cast_specs],
        out_shape=[jax.ShapeDtypeStruct((cfg.r, ncols), out_dtype)] + [s[2] for s in cast_specs],
        compiler_params=_cparams(("arbitrary", "arbitrary"), blocks, single + 2 * _nbytes((cfg.tm, tn), F32)),
        name=name,
    )(x, w, *tails, *[a for a, _, _ in extras], *[c.src for c in casts])
    return outs if casts else outs[0]


def _prompt_taps(x, k):
    row = _rows(x.shape)
    return [x] + [jnp.where(row >= s, _shift_down(x, s), 0.0) for s in range(1, k)]


def _sample_taps(x, e, k, ds):
    t = _rows(x.shape) % ds
    return [x] + [jnp.where(t >= s, _shift_down(x, s), _shift_up(e, k - 1 - s)) for s in range(1, k)]


def _conv_from_taps(taps, w, b=None):
    k = len(taps)
    y = taps[0] * w[k - 1:k]
    for s in range(1, k):
        y = y + taps[s] * w[k - 1 - s:k - s]
    return y if b is None else y + b


def _sconv_prompt_kernel(h_ref, b_ref, c_ref, w_ref, v_ref, st_ref, *, lp):
    ch = c_ref[...].astype(F32) * h_ref[...].astype(F32)
    u = _conv_from_taps(_prompt_taps(ch, 3), w_ref[...])
    v_ref[...] = (b_ref[...].astype(F32) * u).astype(v_ref.dtype)
    st_ref[0] = ch[lp - SUBLANES:lp][SUBLANES - 2:]


def _sconv_sample_kernel(h_ref, b_ref, c_ref, e_ref, w_ref, v_ref, ch_ref, *, ds):
    ch = c_ref[...].astype(F32) * h_ref[...].astype(F32)
    u = _conv_from_taps(_sample_taps(ch, e_ref[...], 3, ds), w_ref[...])
    v_ref[...] = (b_ref[...].astype(F32) * u).astype(v_ref.dtype)
    ch_ref[...] = ch


def _sconv(cfg, proj, e_sc, w):
    d = cfg.d
    tc = _largest_tile(d, 512)
    nb = d // tc
    pblk = lambda off: pl.BlockSpec((cfg.lp, tc), lambda b, j: (b, j + off * nb))
    blk = 4 * _nbytes((cfg.lp, tc), BF16)
    v_p, st_p = pl.pallas_call(
        functools.partial(_sconv_prompt_kernel, lp=cfg.lp),
        grid=(cfg.bp, nb),
        in_specs=[pblk(2), pblk(3), pblk(4), pl.BlockSpec((3, tc), lambda b, j: (0, j))],
        out_specs=[pl.BlockSpec((cfg.lp, tc), lambda b, j: (b, j)),
                   pl.BlockSpec((1, 2, tc), lambda b, j: (b, 0, j))],
        out_shape=[jax.ShapeDtypeStruct((cfg.bp * cfg.lp, d), BF16),
                   jax.ShapeDtypeStruct((cfg.bp, 2, d), F32)],
        compiler_params=_cparams(("parallel", "parallel"), blk, 6 * _nbytes((cfg.lp, tc), F32)),
        name="sconv_prompt",
    )(proj, proj, proj, w)
    sblk = lambda off: pl.BlockSpec((cfg.tm, tc), lambda j: (cfg.sb, j + off * nb))
    own = pl.BlockSpec((cfg.tm, tc), lambda j: (0, j))
    blk = 4 * _nbytes((cfg.tm, tc), BF16) + 2 * _nbytes((cfg.tm, tc), F32)
    v_s, ch_s = pl.pallas_call(
        functools.partial(_sconv_sample_kernel, ds=cfg.ds),
        grid=(nb,),
        in_specs=[sblk(2), sblk(3), sblk(4), own, pl.BlockSpec((3, tc), lambda j: (0, j))],
        out_specs=[own, own],
        out_shape=[jax.ShapeDtypeStruct((cfg.tm, d), BF16),
                   jax.ShapeDtypeStruct((cfg.tm, d), F32)],
        compiler_params=_cparams(("parallel",), blk, 6 * _nbytes((cfg.tm, tc), F32)),
        name="sconv_sample",
    )(proj, proj, proj, e_sc, w)
    st_s = ch_s[:cfg.db * cfg.ds].reshape(cfg.db, cfg.ds, d)[:, cfg.ds - 2:]
    return v_p, v_s, st_p, st_s


def _tile_taps(x, halo, k):
    row8 = _rows((SUBLANES, x.shape[1]))
    taps = [x]
    for s in range(1, k):
        body = _shift_down(x, s)
        top = jnp.where(row8 >= s, body[:SUBLANES], _shift_down(halo, s))
        taps.append(jnp.concatenate([top, body[SUBLANES:]], axis=0))
    return taps


def _ffn_up_kernel(x_ref, wa_ref, wg_ref, ea_ref, eg_ref, cwa_ref, cwg_ref, cba_ref, cbg_ref, *refs, cfg, n_cast):
    h_ref, pst_ref, ups_ref = refs[n_cast:n_cast + 3]
    raw_ref, halo_ref = refs[-2:]
    _run_casts(refs[:n_cast], refs[n_cast + 3:-2])
    i = pl.program_id(0)
    j = pl.program_id(1)
    tm = cfg.tm

    @pl.when(jnp.logical_and(i == 0, j == 0))
    def _():
        raw_ref[...] = jnp.zeros_like(raw_ref)
        halo_ref[...] = jnp.zeros_like(halo_ref)

    def variant(sample):
        x = x_ref[...]
        raw_ref[0, SUBLANES:, :] = _dot(x, wa_ref[...])
        raw_ref[1, SUBLANES:, :] = _dot(x, wg_ref[...])
        lag = lambda half, s: raw_ref[half, pl.ds(SUBLANES - s, tm), :]
        if sample:
            t = _rows((tm, 1)) % cfg.ds
            taps = [[lag(half, 0)] + [jnp.where(t >= s, lag(half, s), e_ref[pl.ds(2 - s, tm), :]) for s in (1, 2)]
                    for half, e_ref in ((0, ea_ref), (1, eg_ref))]
            ups_ref[0] = taps[0][0]
            ups_ref[1] = taps[1][0]
        else:
            inside = i % cfg.nt != 0
            for half in (0, 1):
                raw_ref[half, :SUBLANES, :] = jnp.where(inside, halo_ref[j, half], 0.0)
            taps = [[lag(half, s) for s in range(3)] for half in (0, 1)]
        act = _conv_from_taps(taps[0], cwa_ref[...], cba_ref[...])
        gate = _conv_from_taps(taps[1], cwg_ref[...], cbg_ref[...])
        h_ref[...] = (_silu(act) * gate).astype(h_ref.dtype)
        for half in (0, 1):
            tail = raw_ref[half, tm:, :]
            pst_ref[0, half] = tail[SUBLANES - 2:]
            halo_ref[j, half] = tail

    pl.when(i < cfg.sb)(functools.partial(variant, False))
    pl.when(i == cfg.sb)(functools.partial(variant, True))


def _ffn_up(cfg, xn, w_up, e_fc, cw, cb, casts=()):
    d_ff, tm, sb, nt = cfg.d_ff, cfg.tm, cfg.sb, cfg.nt
    k = xn.shape[1]
    tn = _largest_tile(d_ff, 512)
    nb = d_ff // tn
    ni = cfg.r // tm
    only_sb = lambda i, j: jnp.where(i == sb, j, 0)
    cast_specs = [c.specs(ni * nb, lambda i, j: i * nb + j) for c in casts]
    blk = (_nbytes((tm, k), BF16) + 2 * _nbytes((k, tn), BF16) + 2 * _nbytes((tm, tn), F32)
           + _nbytes((tm, tn), BF16) + 2 * _nbytes((tm, tn), F32) + sum(s[3] for s in cast_specs))
    h, pst, ups, *cast_out = pl.pallas_call(
        functools.partial(_ffn_up_kernel, cfg=cfg, n_cast=len(casts)),
        grid=(ni, nb),
        in_specs=[pl.BlockSpec((tm, k), lambda i, j: (i, 0)),
                  pl.BlockSpec((k, tn), lambda i, j: (0, j)),
                  pl.BlockSpec((k, tn), lambda i, j: (0, j + nb)),
                  pl.BlockSpec((tm + SUBLANES, tn), lambda i, j: (0, only_sb(i, j))),
                  pl.BlockSpec((tm + SUBLANES, tn), lambda i, j: (0, only_sb(i, j) + nb)),
                  pl.BlockSpec((3, tn), lambda i, j: (0, j)), pl.BlockSpec((3, tn), lambda i, j: (0, j + nb)),
                  pl.BlockSpec((1, tn), lambda i, j: (0, j)), pl.BlockSpec((1, tn), lambda i, j: (0, j + nb))]
                 + [s[0] for s in cast_specs],
        out_specs=[pl.BlockSpec((tm, tn), lambda i, j: (i, j)),
                   pl.BlockSpec((1, 2, 2, tn), lambda i, j: (i, 0, 0, j)),
                   pl.BlockSpec((2, tm, tn), lambda i, j: (0, 0, only_sb(i, j)))] + [s[1] for s in cast_specs],
        out_shape=[jax.ShapeDtypeStruct((cfg.r, d_ff), BF16),
                   jax.ShapeDtypeStruct((ni, 2, 2, d_ff), F32),
                   jax.ShapeDtypeStruct((2, tm, d_ff), F32)] + [s[2] for s in cast_specs],
        scratch_shapes=[pltpu.VMEM((2, SUBLANES + tm, tn), F32), pltpu.VMEM((nb, 2, SUBLANES, tn), F32)],
        compiler_params=_cparams(("arbitrary", "arbitrary"), blk, 16 * _nbytes((tm, tn), F32)),
        name="ffn_up",
    )(xn, w_up, w_up, e_fc, e_fc, cw, cw, cb, cb, *[c.src for c in casts])
    last = pst[nt - 1:sb:nt]
    p_fc = jnp.concatenate([last[:, 0], last[:, 1]], axis=-1)
    return h, p_fc, ups, cast_out


def _ssd_gate_norm(y, z, gnw):
    y = y * _silu(z)
    return y * lax.rsqrt(jnp.mean(y * y, axis=-1, keepdims=True) + EPS) * gnw


def _ssd_prompt_kernel(x_ref, bm_ref, cm_ref, z_ref, dtr_ref, wx_ref, wb_ref, wc_ref, bx_ref, bb_ref, bc_ref,
                       dtb_ref, alog_ref, dsk_ref, gnw_ref, e_ref, tril_ref,
                       y_ref, hfin_ref, mcx_ref, mcb_ref, mcc_ref,
                       ht_ref, hx_ref, hb_ref, hc_ref, *, cfg):
    lp, hpg, n, gw = cfg.lp, cfg.hpg, cfg.n, cfg.gw
    c = SSD_CHUNK
    nch = 1 + cfg.seq // c
    ht_ref[...] = jnp.zeros_like(ht_ref)
    hx_ref[...] = jnp.zeros_like(hx_ref)
    hb_ref[...] = jnp.zeros_like(hb_ref)
    hc_ref[...] = jnp.zeros_like(hc_ref)
    a_neg = -jnp.exp(alog_ref[0])
    dtb = dtb_ref[0]
    e = e_ref[...]
    tril = tril_ref[...]
    li = lax.broadcasted_iota(jnp.int32, (c, c), 0)
    si = lax.broadcasted_iota(jnp.int32, (c, c), 1)
    causal = li >= si
    lane_lo = lax.broadcasted_iota(jnp.int32, (c, 2 * HEAD_DIM), 1) < HEAD_DIM

    def conv_silu(src_ref, nxt, buf_ref, w_ref, b_ref, first):
        y = _silu(_conv_from_taps([buf_ref[pl.ds(SUBLANES - s, c), :] for s in range(4)], w_ref[...], b_ref[...]))
        buf_ref[:SUBLANES, :] = jnp.where(first, buf_ref[N_META:N_META + SUBLANES, :], buf_ref[c:, :])
        buf_ref[SUBLANES:, :] = src_ref[nxt, :].astype(F32)
        return y

    for src_ref, buf_ref in ((x_ref, hx_ref), (bm_ref, hb_ref), (cm_ref, hc_ref)):
        buf_ref[SUBLANES:, :] = src_ref[:c, :].astype(F32)

    def chunk(k, carry):
        first = k == 0
        start = pl.multiple_of(jnp.where(first, 0, N_META + c * (k - 1)), BF16_ROWS)
        rows = pl.ds(start, c)
        nxt = pl.ds(pl.multiple_of(jnp.minimum(N_META + c * k, lp - c), BF16_ROWS), c)
        xh = conv_silu(x_ref, nxt, hx_ref, wx_ref, bx_ref, first)
        bc = conv_silu(bm_ref, nxt, hb_ref, wb_ref, bb_ref, first)
        cc = conv_silu(cm_ref, nxt, hc_ref, wc_ref, bc_ref, first)
        live = jnp.logical_or(jnp.logical_not(first), _rows((c, LANES)) < N_META)
        dt = jnp.where(live, _softplus(dtr_ref[rows, :] + dtb), 0.0)
        a = dt * a_neg
        a1 = a.astype(BF16)
        r1 = a - a1.astype(F32)
        a2 = r1.astype(BF16)
        a3 = (r1 - a2.astype(F32)).astype(BF16)
        cs = _dot(tril, a1) + _dot(tril, a2) + _dot(tril, a3)
        cs_t = cs.T
        cs_last = cs[c - 1:c]
        ecs = jnp.exp(cs)
        both_e = _dot(jnp.concatenate([dt, dt * jnp.exp(cs_last - cs)], axis=0).astype(BF16), e)
        xs_b = (xh * both_e[:c]).astype(BF16)
        xsw_b = (xh * both_e[c:]).astype(BF16)
        dec_row = _expand_heads(jnp.broadcast_to(ecs[c - 1:c], (SUBLANES, LANES)), e)[:1]
        scores = jnp.where(causal, _dot_nt(cc.astype(BF16), bc.astype(BF16)), 0.0)
        ht = ht_ref[...]
        ht_b = ht.astype(BF16)
        pairs = []
        for q in range(hpg // 2):
            cols = slice(2 * HEAD_DIM * q, 2 * HEAD_DIM * (q + 1))
            rhs = jnp.concatenate([xs_b[:, cols], ht_b[:, cols]], axis=0)
            res = []
            for r in (2 * q, 2 * q + 1):
                m = scores * jnp.exp(jnp.minimum(cs[:, r:r + 1] - cs_t[r:r + 1, :], 0.0))
                lhs = jnp.concatenate([m, cc * ecs[:, r:r + 1]], axis=1).astype(BF16)
                res.append(_dot(lhs, rhs))
            pairs.append(jnp.where(lane_lo, res[0], res[1]))
        y = jnp.concatenate(pairs, axis=1) if len(pairs) > 1 else pairs[0]
        ht_ref[...] = ht * dec_row + _dot(bc.T.astype(BF16), xsw_b)
        y = y + dsk_ref[...] * xh
        y_ref[rows, :] = _ssd_gate_norm(y, z_ref[rows, :].astype(F32), gnw_ref[...]).astype(y_ref.dtype)
        return carry

    lax.fori_loop(0, nch, chunk, 0)
    hfin_ref[0] = ht_ref[...].T.reshape(hpg, HEAD_DIM, n)
    tail = lambda ref: ref[lp - BF16_ROWS:lp, :].astype(F32)[BF16_ROWS - 3:]
    mcx_ref[0] = tail(x_ref)
    mcb_ref[0] = tail(bm_ref)
    mcc_ref[0] = tail(cm_ref)


def _ssd_prompt(cfg, proj, dtr_g, prm):
    lp, gw, n, hpg, g_ = cfg.lp, cfg.gw, cfg.n, cfg.hpg, N_GROUPS
    xb0 = 7 * cfg.d // gw
    zb0 = 5 * cfg.d // gw
    bb0 = 9 * cfg.d // n
    cvb = cfg.d_inner // n
    colblk = lambda w, f: pl.BlockSpec((lp, w), f)
    par = lambda rows, w, f: pl.BlockSpec((rows, w), f)
    vec3 = pl.BlockSpec((1, 1, LANES), lambda b, g: (g, 0, 0))
    in_specs = [
        colblk(gw, lambda b, g: (b, xb0 + g)), colblk(n, lambda b, g: (b, bb0 + g)),
        colblk(n, lambda b, g: (b, bb0 + g_ + g)), colblk(gw, lambda b, g: (b, zb0 + g)),
        colblk(LANES, lambda b, g: (b, g)),
        par(4, gw, lambda b, g: (0, g)), par(4, n, lambda b, g: (0, cvb + g)), par(4, n, lambda b, g: (0, cvb + g_ + g)),
        par(1, gw, lambda b, g: (0, g)), par(1, n, lambda b, g: (0, cvb + g)), par(1, n, lambda b, g: (0, cvb + g_ + g)),
        vec3, vec3,
        par(1, gw, lambda b, g: (0, g)), par(1, gw, lambda b, g: (0, g)),
        par(LANES, gw, lambda b, g: (0, 0)), par(SSD_CHUNK, SSD_CHUNK, lambda b, g: (0, 0)),
    ]
    out_specs = [
        pl.BlockSpec((lp, gw), lambda b, g: (b, g)),
        pl.BlockSpec((1, hpg, HEAD_DIM, n), lambda b, g: (b, g, 0, 0)),
        pl.BlockSpec((1, 3, gw), lambda b, g: (b, 0, g)),
        pl.BlockSpec((1, 3, n), lambda b, g: (b, 0, g)),
        pl.BlockSpec((1, 3, n), lambda b, g: (b, 0, g)),
    ]
    out_shape = [
        jax.ShapeDtypeStruct((cfg.bp * cfg.lp, cfg.d_inner), BF16),
        jax.ShapeDtypeStruct((cfg.bp, cfg.h, HEAD_DIM, n), F32),
        jax.ShapeDtypeStruct((cfg.bp, 3, cfg.d_inner), F32),
        jax.ShapeDtypeStruct((cfg.bp, 3, g_ * n), F32),
        jax.ShapeDtypeStruct((cfg.bp, 3, g_ * n), F32),
    ]
    blk = 3 * _nbytes((lp, gw), BF16) + 2 * _nbytes((lp, n), BF16) + _nbytes((lp, LANES), F32)
    y, hfin, mcx, mcb, mcc = pl.pallas_call(
        functools.partial(_ssd_prompt_kernel, cfg=cfg),
        grid=(cfg.bp, g_),
        in_specs=in_specs, out_specs=out_specs, out_shape=out_shape,
        scratch_shapes=[pltpu.VMEM((n, gw), F32), pltpu.VMEM((SUBLANES + SSD_CHUNK, gw), F32),
                        pltpu.VMEM((SUBLANES + SSD_CHUNK, n), F32), pltpu.VMEM((SUBLANES + SSD_CHUNK, n), F32)],
        compiler_params=_cparams(("parallel", "parallel"), blk, 24 * _nbytes((SSD_CHUNK, gw), F32) + (4 << 20)),
        name="ssd_prompt",
    )(proj, proj, proj, proj, dtr_g,
      prm["mconv_w"], prm["mconv_w"], prm["mconv_w"], prm["mconv_b"], prm["mconv_b"], prm["mconv_b"],
      prm["dt_bias_g"], prm["a_log_g"], prm["d_skip_e"], prm["gnorm_w"], prm["expand"], prm["tril"])
    return y, hfin, jnp.concatenate([mcx, mcb, mcc], axis=-1)


def _ssd_tok_kernel(x_ref, bm_ref, cm_ref, dtr_ref, ex_ref, eb_ref, ec_ref, wx_ref, wb_ref, wc_ref,
                    bx_ref, bb_ref, bc_ref, dtb_ref, alog_ref, dsk_ref, e_ref,
                    ypre_ref, ecs_ref, xsw_ref, bo_ref, co_ref, dec_ref, *, ds):
    e = e_ref[...]
    conv = lambda ref, eref, w, b: _silu(_conv_from_taps(
        _sample_taps(ref[...].astype(F32), eref[...], 4, ds), w[...], b[...]))
    xh = conv(x_ref, ex_ref, wx_ref, bx_ref)
    bc = conv(bm_ref, eb_ref, wb_ref, bb_ref)
    cc = conv(cm_ref, ec_ref, wc_ref, bc_ref)
    dt = _softplus(dtr_ref[...] + dtb_ref[0])
    a = dt * (-jnp.exp(alog_ref[0]))
    t = _rows(a.shape) % ds
    t1 = _rows((a.shape[0], 1)) % ds
    cs = a
    for j in range(1, ds):
        cs = cs + jnp.where(t >= j, _shift_down(a, j), 0.0)
    cs_last = jnp.zeros_like(cs)
    for j in range(ds):
        cs_last = cs_last + jnp.where(t == ds - 1 - j, _shift_up(cs, j), 0.0)
    ecs_e = _expand_heads(jnp.exp(cs), e)
    xs = xh * _expand_heads(dt, e)
    y = dsk_ref[...] * xh
    for j in range(ds):
        sc = jnp.sum(cc * _shift_down(bc, j), axis=-1, keepdims=True)
        diff = jnp.where(t >= j, cs - _shift_down(cs, j), 0.0)
        coef = jnp.where(t1 >= j, sc, 0.0) * jnp.exp(diff)
        y = y + _expand_heads(coef, e) * _shift_down(xs, j)
    ypre_ref[...] = y
    ecs_ref[...] = ecs_e
    xsw_ref[...] = xs * _expand_heads(jnp.exp(cs_last - cs), e)
    bo_ref[...] = bc
    co_ref[...] = cc
    dec_ref[...] = jnp.exp(cs_last)


def _ssd_state_kernel(dec_ref, h0_ref, c_ref, b_ref, xsw_ref, *refs, cfg):
    hout_ref, yoff_ref = refs[-2:]
    hpg, n, gw, ds = cfg.hpg, cfg.n, cfg.gw, cfg.ds
    i = pl.program_id(0)
    nseq = SUBLANES // ds
    pad = jnp.zeros((LANES - SUBLANES, n), F32)
    row8 = _rows((SUBLANES, 1)) // ds
    cmat = c_ref[...]
    bmat = b_ref[...]
    xsw = xsw_ref[...]
    youts = []
    for g in range(N_GROUPS):
        cg = cmat[:, g * n:(g + 1) * n].astype(BF16)
        bg = bmat[:, g * n:(g + 1) * n]
        xw = jnp.concatenate([xsw[:, g * gw:(g + 1) * gw], jnp.zeros((LANES - SUBLANES, gw), F32)], axis=0)
        xw_t = xw.T.astype(BF16)
        yg = jnp.zeros((SUBLANES, gw), F32)
        for s in range(nseq):
            h0 = h0_ref[s, g * hpg:(g + 1) * hpg].reshape(gw, n)
            yg = jnp.where(row8 == s, _dot_nt(cg, h0.astype(BF16)), yg)
            bs = jnp.concatenate([jnp.where(row8 == s, bg, 0.0), pad], axis=0).astype(BF16)
            dh = _dot(xw_t, bs)
            for r in range(hpg):
                sl = slice(r * HEAD_DIM, (r + 1) * HEAD_DIM)
                hout_ref[s, g * hpg + r] = h0[sl] * dec_ref[i * nseq + s, g * hpg + r] + dh[sl]
        youts.append(yg)
    yoff_ref[...] = jnp.concatenate(youts, axis=1)


def _ssd_post_kernel(ypre_ref, yoff_ref, ecs_ref, z_ref, gnw_ref, o_ref):
    y = ypre_ref[...] + yoff_ref[...] * ecs_ref[...]
    o_ref[...] = _ssd_gate_norm(y, z_ref[...].astype(F32), gnw_ref[...]).astype(o_ref.dtype)


def _ssd_sample(cfg, proj, dtr_g, e_mc, h_all, layer, h_new, prm):
    tm, gw, n, g_, sb = cfg.tm, cfg.gw, cfg.n, N_GROUPS, cfg.sb
    xb0 = 7 * cfg.d // gw
    zb0 = 5 * cfg.d // gw
    bb0 = 9 * cfg.d // n
    cvb = cfg.d_inner // n
    row = lambda w, f: pl.BlockSpec((tm, w), f)
    par = lambda rows, w, f: pl.BlockSpec((rows, w), f)
    vec3 = pl.BlockSpec((1, 1, LANES), lambda g: (g, 0, 0))
    in_specs = [
        row(gw, lambda g: (sb, xb0 + g)), row(n, lambda g: (sb, bb0 + g)), row(n, lambda g: (sb, bb0 + g_ + g)),
        row(LANES, lambda g: (sb, g)),
        row(gw, lambda g: (0, g)), row(n, lambda g: (0, cvb + g)), row(n, lambda g: (0, cvb + g_ + g)),
        par(4, gw, lambda g: (0, g)), par(4, n, lambda g: (0, cvb + g)), par(4, n, lambda g: (0, cvb + g_ + g)),
        par(1, gw, lambda g: (0, g)), par(1, n, lambda g: (0, cvb + g)), par(1, n, lambda g: (0, cvb + g_ + g)),
        vec3, vec3, par(1, gw, lambda g: (0, g)), par(LANES, gw, lambda g: (0, 0)),
    ]
    own = lambda w: pl.BlockSpec((tm, w), lambda g: (0, g))
    blk = 2 * _nbytes((tm, gw), BF16) + 5 * _nbytes((tm, gw), F32)
    ypre, ecs, xsw, bo, co, dec = pl.pallas_call(
        functools.partial(_ssd_tok_kernel, ds=cfg.ds),
        grid=(g_,),
        in_specs=in_specs,
        out_specs=[own(gw), own(gw), own(gw), own(n), own(n), own(LANES)],
        out_shape=[jax.ShapeDtypeStruct((tm, cfg.d_inner), F32)] * 3
                  + [jax.ShapeDtypeStruct((tm, g_ * n), F32)] * 2
                  + [jax.ShapeDtypeStruct((tm, g_ * LANES), F32)],
        compiler_params=_cparams(("parallel",), blk, 16 * _nbytes((tm, gw), F32)),
        name="ssd_tok",
    )(proj, proj, proj, dtr_g, e_mc, e_mc, e_mc,
      prm["mconv_w"], prm["mconv_w"], prm["mconv_w"], prm["mconv_b"], prm["mconv_b"], prm["mconv_b"],
      prm["dt_bias_g"], prm["a_log_g"], prm["d_skip_e"], prm["expand"])
    nrow = cfg.db * cfg.ds
    dec_sh = dec[:nrow].reshape(cfg.db, cfg.ds, g_, LANES)[:, cfg.ds - 1, :, :cfg.hpg].reshape(cfg.db, cfg.h)
    nseq = SUBLANES // cfg.ds
    rows8 = lambda w: pl.BlockSpec((SUBLANES, w), lambda i: (i, 0))
    hblk = pl.BlockSpec((None, nseq, cfg.h, HEAD_DIM, n), lambda i: (layer, i, 0, 0, 0))
    blk = 2 * _nbytes((nseq, cfg.h, HEAD_DIM, n), F32) + 3 * _nbytes((SUBLANES, cfg.d_inner), F32)
    hbm = pl.BlockSpec(memory_space=pl.ANY)
    aliased = [jnp.zeros((tm, cfg.d_inner), F32)] + ([] if h_new is None else [h_new])
    hout, yoff = pl.pallas_call(
        functools.partial(_ssd_state_kernel, cfg=cfg),
        grid=(cfg.db // nseq,),
        in_specs=[pl.BlockSpec(memory_space=pltpu.SMEM), hblk, rows8(g_ * n), rows8(g_ * n), rows8(cfg.d_inner)]
                 + [hbm] * len(aliased),
        out_specs=[hblk, rows8(cfg.d_inner)],
        out_shape=[jax.ShapeDtypeStruct(h_all.shape, F32), jax.ShapeDtypeStruct((tm, cfg.d_inner), F32)],
        input_output_aliases={5: 1} if h_new is None else {5: 1, 6: 0},
        compiler_params=_cparams(("arbitrary",), blk, 8 << 20),
        name="ssd_state",
    )(dec_sh, h_all, co, bo, xsw, *aliased)
    blk = 3 * _nbytes((tm, gw), F32) + 2 * _nbytes((tm, gw), BF16)
    yg = pl.pallas_call(
        _ssd_post_kernel,
        grid=(g_,),
        in_specs=[own(gw), own(gw), own(gw), row(gw, lambda g: (sb, zb0 + g)), par(1, gw, lambda g: (0, g))],
        out_specs=own(gw),
        out_shape=jax.ShapeDtypeStruct((tm, cfg.d_inner), BF16),
        compiler_params=_cparams(("parallel",), blk, 4 * _nbytes((tm, gw), F32)),
        name="ssd_post",
    )(ypre, yoff, ecs, proj, prm["gnorm_w"])
    return yg, hout


def _flat_state(cfg, buf):
    db, km1, c = buf.shape
    nseq = (cfg.tm + SUBLANES) // cfg.ds
    assert nseq * cfg.ds == cfg.tm + SUBLANES
    return jnp.pad(buf.astype(F32), ((0, nseq - db), (0, cfg.ds - km1), (0, 0))).reshape(nseq * cfg.ds, c)


def _dt_weights(cfg, w_in):
    depth, g_, hpg = w_in.shape[0], N_GROUPS, cfg.hpg
    w_dt = w_in[:, :, cfg.nmain:].reshape(depth, cfg.d, g_, hpg)
    w_dt = jnp.pad(w_dt, ((0, 0), (0, 0), (0, 0), (0, LANES - hpg))).reshape(depth, cfg.d, g_ * LANES)
    return w_dt.astype(BF16)


def _layer_params(cfg, sconv_w, mconv_w, mconv_b, dt_bias, a_log, d_skip, gnorm_w, fconv_w, fconv_b):
    g_, hpg = N_GROUPS, cfg.hpg
    grp = lambda v: jnp.pad(v.astype(F32).reshape(g_, 1, hpg), ((0, 0), (0, 0), (0, LANES - hpg)))
    head = jnp.arange(LANES)[:, None]
    col = jnp.arange(cfg.gw)[None, :] // HEAD_DIM
    return dict(
        sconv_w=sconv_w.astype(F32),
        mconv_w=mconv_w.astype(F32), mconv_b=mconv_b.astype(F32).reshape(1, -1),
        dt_bias_g=grp(dt_bias), a_log_g=grp(a_log),
        d_skip_e=jnp.repeat(d_skip.astype(F32), HEAD_DIM).reshape(1, -1),
        gnorm_w=gnorm_w.astype(F32).reshape(1, -1),
        fconv_w=fconv_w.astype(F32), fconv_b=fconv_b.astype(F32).reshape(1, -1),
        expand=((head == col) & (head < hpg)).astype(BF16),
        tril=(jnp.arange(SSD_CHUNK)[:, None] >= jnp.arange(SSD_CHUNK)[None, :]).astype(BF16),
    )


def _layer(cfg, x, xn, w_in_b, w_dt_b, f32_mats, layer, prm, st_sc, st_mc, h_all, h_new, st_fc):
    d, d_ff = cfg.d, cfg.d_ff
    nrow = cfg.db * cfg.ds
    srows = slice(cfg.sb * cfg.tm, cfg.sb * cfg.tm + nrow)
    own = ("w_up", "w_down", "wa_out", "wb_out", "w_o")
    proj, *cast = _mm(cfg, xn, w_in_b, cfg.nmain, BF16, "proj", casts=[Cast(f32_mats[n], layer) for n in own])
    mats = dict(zip(own, cast))
    dtr_g = _mm(cfg, xn, w_dt_b, N_GROUPS * LANES, F32, "proj_dt")
    v, v_s, p_sc, s_sc = _sconv(cfg, proj, _flat_state(cfg, st_sc), prm["sconv_w"])
    out_a = _mm(cfg, v, mats["wa_out"], d, BF16, "wa_out", x_tail=v_s)
    yg, p_h, p_mc = _ssd_prompt(cfg, proj, dtr_g, prm)
    yg_s, h_new = _ssd_sample(cfg, proj, dtr_g, _flat_state(cfg, st_mc), h_all, layer, h_new, prm)
    s_mc = proj[srows, 7 * d:7 * d + cfg.conv_dim].astype(F32).reshape(cfg.db, cfg.ds, cfg.conv_dim)[:, cfg.ds - 3:]
    mix = _mm(cfg, yg, mats["wb_out"], d, BF16, "wb_out", epilogue="mix", x_tail=yg_s,
              extras=((proj, 0, BF16), (proj, 1, BF16), (out_a, 0, BF16)))
    x = _mm(cfg, mix, mats["w_o"], d, F32, "w_o", epilogue="resid", extras=((x, 0, F32),))
    xn2 = _rmsnorm(cfg, x, prm["norm2_g"], BF16)
    more = layer + 1 < f32_mats["w_in"].shape[0]
    hmid, p_fc, ups, nxt = _ffn_up(cfg, xn2, mats["w_up"], _flat_state(cfg, st_fc), prm["fconv_w"], prm["fconv_b"],
                                   casts=[Cast(f32_mats["w_in"], layer + 1)] if more else [])
    s_fc = jnp.concatenate([ups[0, :nrow], ups[1, :nrow]], axis=-1).reshape(cfg.db, cfg.ds, 2 * d_ff)[:, cfg.ds - 2:]
    x = _mm(cfg, hmid, mats["w_down"], d, F32, "w_down", epilogue="resid", extras=((x, 0, F32),))
    return x, h_new, (p_sc, p_mc, p_h, p_fc, s_sc, s_mc, s_fc), (nxt[0] if more else None)


def kernel(x_prompt, x_sample, state_sconv, state_mconv, state_ssm, state_fconv, meta_tokens, norm1_g, w_in,
           sconv_w, wa_out, mconv_w, mconv_b, dt_bias, a_log, d_skip, gnorm_w, wb_out, w_o, norm2_g, w_up,
           fconv_w, fconv_b, w_down, final_g):
    cfg = _make_cfg(x_prompt, x_sample, state_ssm, w_up)
    depth = w_in.shape[0]
    dtype = x_prompt.dtype
    nrow = cfg.db * cfg.ds
    meta = meta_tokens.astype(dtype)
    x = jnp.concatenate([piece for b in range(cfg.bp) for piece in (meta, x_prompt[b])]
                        + [x_sample.reshape(nrow, cfg.d),
                           jnp.zeros((cfg.r - cfg.bp * cfg.lp - nrow, cfg.d), dtype)], axis=0).astype(F32)
    f32_mats = dict(w_in=w_in, wa_out=wa_out, wb_out=wb_out, w_o=w_o, w_up=w_up, w_down=w_down)
    w_dt_b = _dt_weights(cfg, w_in)
    w_in_b = _cast_layer(w_in, 0)
    states = []
    h_new = None
    for l in range(depth):
        prm = _layer_params(cfg, sconv_w[l], mconv_w[l], mconv_b[l], dt_bias[l], a_log[l], d_skip[l], gnorm_w[l],
                            fconv_w[l], fconv_b[l])
        prm["norm2_g"] = norm2_g[l]
        xn = _rmsnorm(cfg, x, norm1_g[l], BF16)
        x, h_new, st, w_in_b = _layer(cfg, x, xn, w_in_b, w_dt_b[l], f32_mats, l, prm, state_sconv[l],
                                      state_mconv[l], state_ssm, h_new, state_fconv[l])
        states.append(st)
    y_p = _rmsnorm(cfg, x, final_g, F32, 0, cfg.bp * cfg.lp)
    y_s = _rmsnorm(cfg, x, final_g, F32, cfg.sb * cfg.tm, cfg.tm)
    y_prompt = y_p.reshape(cfg.bp, cfg.lp, cfg.d)[:, N_META:].astype(dtype)
    y_sample = y_s[:nrow].reshape(cfg.db, cfg.ds, cfg.d).astype(dtype)
    p_sc, p_mc, p_h, p_fc, s_sc, s_mc, s_fc = [jnp.stack([s[i] for s in states]).astype(dtype) for i in range(7)]
    return (y_prompt, y_sample, p_sc, p_mc, p_h, p_fc, s_sc, s_mc, h_new.astype(dtype), s_fc)
```

```python
import functools
import math
from typing import NamedTuple

import jax
import jax.numpy as jnp
from jax import lax
from jax.experimental import pallas as pl
from jax.experimental.pallas import tpu as pltpu

F32 = jnp.float32
BF16 = jnp.bfloat16

N_META = 16
N_GROUPS = 8
HEAD_DIM = 64
SSD_CHUNK = 128
EPS = 1e-6

LANES = 128
SUBLANES = 8
BF16_ROWS = 16
VMEM_PHYSICAL_BYTES = 64 * 1024 * 1024
VMEM_BUDGET_BYTES = 56 * 1024 * 1024
COMPILER_SCRATCH_BYTES = 6 * 1024 * 1024
NORM_ROWS = 256
CAST_CHUNK_BYTES = 8 * 1024 * 1024
MM_VMEM_BYTES = 48 * 1024 * 1024


class Cfg(NamedTuple):
    d: int
    bp: int
    seq: int
    lp: int
    db: int
    ds: int
    h: int
    hpg: int
    gw: int
    n: int
    d_inner: int
    conv_dim: int
    d_ff: int
    nt: int
    tm: int
    sb: int
    r: int
    nmain: int


def _largest_tile(dim, target, quantum=LANES):
    best = None
    t = quantum
    while t <= min(dim, target):
        if dim % t == 0:
            best = t
        t += quantum
    assert best is not None, (dim, target, quantum)
    return best


def _make_cfg(x_prompt, x_sample, state_ssm, w_up):
    bp, seq, d = x_prompt.shape
    db, ds, _ = x_sample.shape
    _, _, h, p, n = state_ssm.shape
    assert p == HEAD_DIM and h % N_GROUPS == 0 and seq % SSD_CHUNK == 0
    hpg = h // N_GROUPS
    assert hpg % 2 == 0 and hpg <= LANES
    d_inner = h * HEAD_DIM
    lp = N_META + seq
    d_ff = w_up.shape[-1] // 2
    nt = 3 if (lp % (3 * BF16_ROWS) == 0 and lp // 3 >= db * ds) else 1
    tm = lp // nt
    assert tm % BF16_ROWS == 0 and db * ds <= tm and ds >= 3
    sb = bp * nt
    return Cfg(d=d, bp=bp, seq=seq, lp=lp, db=db, ds=ds, h=h, hpg=hpg, gw=hpg * HEAD_DIM, n=n,
               d_inner=d_inner, conv_dim=d_inner + 2 * N_GROUPS * n, d_ff=d_ff, nt=nt, tm=tm,
               sb=sb, r=(sb + 1) * tm, nmain=5 * d + d_inner + d_inner + 2 * N_GROUPS * n)


def _cparams(semantics, block_bytes, temp_bytes=0):
    need = 2 * block_bytes + temp_bytes + COMPILER_SCRATCH_BYTES
    return pltpu.CompilerParams(dimension_semantics=semantics,
                                vmem_limit_bytes=int(min(max(need, 16 << 20), VMEM_BUDGET_BYTES)))


def _nbytes(shape, dtype):
    return math.prod(shape) * jnp.dtype(dtype).itemsize


def _silu(x):
    half = 0.5 * x
    return half + half * jnp.tanh(half)


def _softplus(x):
    return jnp.maximum(x, 0.0) + jnp.log1p(jnp.exp(-jnp.abs(x)))


def _shift_down(x, s):
    return x if s == 0 else pltpu.roll(x, s, axis=0)


def _shift_up(x, s):
    return x if s == 0 else pltpu.roll(x, x.shape[0] - s, axis=0)


def _rows(shape):
    return lax.broadcasted_iota(jnp.int32, shape, 0)


def _dot(a, b):
    return jnp.dot(a, b, preferred_element_type=F32)


def _dot_nt(a, b):
    return lax.dot_general(a, b, (((1,), (1,)), ((), ())), preferred_element_type=F32)


def _group_lanes(v, g, hpg):
    return pltpu.roll(v, (LANES - hpg * g) % LANES, axis=1)


def _expand_heads(v, e):
    hi = v.astype(BF16)
    lo = (v - hi.astype(F32)).astype(BF16)
    return _dot(hi, e) + _dot(lo, e)


def _rmsnorm_kernel(x_ref, g_ref, o_ref):
    x = x_ref[...]
    y = x * lax.rsqrt(jnp.mean(x * x, axis=-1, keepdims=True) + EPS)
    o_ref[...] = (y * g_ref[...]).astype(o_ref.dtype)


def _rmsnorm(cfg, x, g, out_dtype, row0=0, nrows=None):
    nrows = cfg.r if nrows is None else nrows
    quantum = BF16_ROWS if jnp.dtype(out_dtype).itemsize < 4 else SUBLANES
    tr = _largest_tile(math.gcd(nrows, row0) if row0 else nrows, NORM_ROWS if row0 == 0 else 2 * NORM_ROWS, quantum)
    blk = _nbytes((tr, cfg.d), F32) + _nbytes((tr, cfg.d), out_dtype)
    b0 = row0 // tr
    return pl.pallas_call(
        _rmsnorm_kernel,
        grid=(nrows // tr,),
        in_specs=[pl.BlockSpec((tr, cfg.d), lambda i: (i + b0, 0)),
                  pl.BlockSpec((1, cfg.d), lambda i: (0, 0))],
        out_specs=pl.BlockSpec((tr, cfg.d), lambda i: (i, 0)),
        out_shape=jax.ShapeDtypeStruct((nrows, cfg.d), out_dtype),
        compiler_params=_cparams(("parallel",), blk, _nbytes((tr, cfg.d), F32)),
        name="rmsnorm",
    )(x, g.reshape(1, cfg.d))


def _rmsnorm_prompt_tokens(cfg, x, g):
    tr = _largest_tile(cfg.seq, NORM_ROWS, SUBLANES)
    blk = 2 * _nbytes((tr, cfg.d), F32)
    return pl.pallas_call(
        _rmsnorm_kernel,
        grid=(cfg.bp, cfg.seq // tr),
        in_specs=[pl.BlockSpec((pl.Element(tr), pl.Element(cfg.d)),
                               lambda b, t: (pl.multiple_of(b * cfg.lp + N_META + t * tr, SUBLANES), 0)),
                  pl.BlockSpec((1, cfg.d), lambda b, t: (0, 0))],
        out_specs=pl.BlockSpec((None, tr, cfg.d), lambda b, t: (b, t, 0)),
        out_shape=jax.ShapeDtypeStruct((cfg.bp, cfg.seq, cfg.d), F32),
        compiler_params=_cparams(("parallel", "parallel"), blk, _nbytes((tr, cfg.d), F32)),
        name="rmsnorm_out",
    )(x, g.reshape(1, cfg.d))


def _cast_kernel(s_ref, d_ref):
    d_ref[...] = s_ref[...].astype(d_ref.dtype)


def _cast_layer(w, layer):
    _, rows, cols = w.shape
    r = _largest_tile(rows, max(BF16_ROWS, CAST_CHUNK_BYTES // (cols * 6)), BF16_ROWS)
    return pl.pallas_call(
        _cast_kernel,
        grid=(rows // r,),
        in_specs=[pl.BlockSpec((None, r, cols), lambda i: (layer, i, 0))],
        out_specs=pl.BlockSpec((r, cols), lambda i: (i, 0)),
        out_shape=jax.ShapeDtypeStruct((rows, cols), BF16),
        compiler_params=_cparams(("parallel",), _nbytes((r, cols), F32) + _nbytes((r, cols), BF16)),
        name="cast_bf16",
    )(w)


class Cast(NamedTuple):
    src: jax.Array
    layer: int

    def chunk_rows(self, nsteps):
        rows = self.src.shape[1]
        return next(r for r in range(BF16_ROWS, rows + 1, BF16_ROWS) if rows % r == 0 and rows // r <= nsteps)

    def specs(self, nsteps, step_of):
        _, rows, cols = self.src.shape
        r = self.chunk_rows(nsteps)
        last = rows // r - 1
        idx = lambda *g: jnp.minimum(step_of(*g), last)
        return (pl.BlockSpec((None, r, cols), lambda *g: (self.layer, idx(*g), 0)),
                pl.BlockSpec((r, cols), lambda *g: (idx(*g), 0)),
                jax.ShapeDtypeStruct((rows, cols), BF16),
                _nbytes((r, cols), F32) + _nbytes((r, cols), BF16))


def _run_casts(src_refs, dst_refs):
    for s, d in zip(src_refs, dst_refs):
        d[...] = s[...].astype(d.dtype)


def _mm_kernel(x_ref, w_ref, *refs, epilogue, n_extra, n_cast, has_tail):
    if has_tail:
        xt_ref, refs = refs[0], refs[1:]
    o_ref = refs[n_extra + n_cast]
    _run_casts(refs[n_extra:n_extra + n_cast], refs[n_extra + n_cast + 1:])

    def product(lhs_ref):
        acc = _dot(lhs_ref[...], w_ref[...])
        if epilogue == "plain":
            o_ref[...] = acc.astype(o_ref.dtype)
        elif epilogue == "resid":
            o_ref[...] = refs[0][...] + acc
        elif epilogue == "mix":
            ga_ref, gb_ref, a_ref = refs[:3]
            ga = jax.nn.sigmoid(ga_ref[...].astype(F32))
            gb = jax.nn.sigmoid(gb_ref[...].astype(F32))
            o_ref[...] = (ga * a_ref[...].astype(F32) + gb * acc).astype(o_ref.dtype)
        else:
            raise ValueError(epilogue)

    if has_tail:
        last = pl.num_programs(0) - 1
        pl.when(pl.program_id(0) < last)(functools.partial(product, x_ref))
        pl.when(pl.program_id(0) == last)(functools.partial(product, xt_ref))
    else:
        product(x_ref)


def _mm(cfg, x, w, ncols, out_dtype, name, epilogue="plain", extras=(), casts=(), x_tail=None):
    k = x.shape[1]
    per_elem = jnp.dtype(out_dtype).itemsize + sum(jnp.dtype(dt).itemsize for _, _, dt in extras)
    ni = cfg.r // cfg.tm
    tails = [] if x_tail is None else [x_tail]
    xrow = (lambda i: i) if x_tail is None else (lambda i: jnp.minimum(i, ni - 2))

    def plan(tn):
        nb = ncols // tn
        cast_specs = [c.specs(ni * nb, lambda i, j: i * nb + j) for c in casts]
        blocks = (_nbytes((cfg.tm, k), BF16) + _nbytes((k, tn), BF16) + cfg.tm * tn * per_elem
                  + sum(s[3] for s in cast_specs))
        return nb, cast_specs, blocks

    single = len(tails) * _nbytes((cfg.tm, k), BF16)
    tn = next(t for t in (1024, 512, 256, LANES)
              if ncols % t == 0 and 2 * plan(t)[2] + single + 2 * _nbytes((cfg.tm, t), F32) <= MM_VMEM_BYTES)
    nb, cast_specs, blocks = plan(tn)
    tile = lambda off: pl.BlockSpec((cfg.tm, tn), lambda i, j: (i, j + off * nb))
    outs = pl.pallas_call(
        functools.partial(_mm_kernel, epilogue=epilogue, n_extra=len(extras), n_cast=len(casts),
                          has_tail=bool(tails)),
        grid=(ni, nb),
        in_specs=[pl.BlockSpec((cfg.tm, k), lambda i, j: (xrow(i), 0)),
                  pl.BlockSpec((k, tn), lambda i, j: (0, j))]
                 + [pl.BlockSpec((cfg.tm, k), lambda i, j: (0, 0), pipeline_mode=pl.Buffered(1)) for _ in tails]
                 + [tile(off) for _, off, _ in extras] + [s[0] for s in cast_specs],
        out_specs=[tile(0)] + [s[1] for s in cast_specs],
        out_shape=[jax.ShapeDtypeStruct((cfg.r, ncols), out_dtype)] + [s[2] for s in cast_specs],
        compiler_params=_cparams(("arbitrary", "arbitrary"), blocks, single + 2 * _nbytes((cfg.tm, tn), F32)),
        name=name,
    )(x, w, *tails, *[a for a, _, _ in extras], *[c.src for c in casts])
    return outs if casts else outs[0]


def _prompt_taps(x, k):
    row = _rows(x.shape)
    return [x] + [jnp.where(row >= s, _shift_down(x, s), 0.0) for s in range(1, k)]


def _sample_taps(x, e, k, ds):
    t = _rows(x.shape) % ds
    return [x] + [jnp.where(t >= s, _shift_down(x, s), _shift_up(e, k - 1 - s)) for s in range(1, k)]


def _conv_from_taps(taps, w, b=None):
    k = len(taps)
    y = taps[0] * w[k - 1:k]
    for s in range(1, k):
        y = y + taps[s] * w[k - 1 - s:k - s]
    return y if b is None else y + b


def _sconv_prompt_kernel(h_ref, b_ref, c_ref, w_ref, v_ref, st_ref, *, lp):
    ch = c_ref[...].astype(F32) * h_ref[...].astype(F32)
    u = _conv_from_taps(_prompt_taps(ch, 3), w_ref[...])
    v_ref[...] = (b_ref[...].astype(F32) * u).astype(v_ref.dtype)
    st_ref[0] = ch[lp - SUBLANES:lp][SUBLANES - 2:]


def _sconv_sample_kernel(h_ref, b_ref, c_ref, e_ref, w_ref, v_ref, ch_ref, *, ds):
    ch = c_ref[...].astype(F32) * h_ref[...].astype(F32)
    u = _conv_from_taps(_sample_taps(ch, e_ref[...], 3, ds), w_ref[...])
    v_ref[...] = (b_ref[...].astype(F32) * u).astype(v_ref.dtype)
    ch_ref[...] = ch


def _sconv(cfg, proj, e_sc, w):
    d = cfg.d
    tc = _largest_tile(d, 512)
    nb = d // tc
    pblk = lambda off: pl.BlockSpec((cfg.lp, tc), lambda b, j: (b, j + off * nb))
    blk = 4 * _nbytes((cfg.lp, tc), BF16)
    v_p, st_p = pl.pallas_call(
        functools.partial(_sconv_prompt_kernel, lp=cfg.lp),
        grid=(cfg.bp, nb),
        in_specs=[pblk(2), pblk(3), pblk(4), pl.BlockSpec((3, tc), lambda b, j: (0, j))],
        out_specs=[pl.BlockSpec((cfg.lp, tc), lambda b, j: (b, j)),
                   pl.BlockSpec((1, 2, tc), lambda b, j: (b, 0, j))],
        out_shape=[jax.ShapeDtypeStruct((cfg.bp * cfg.lp, d), BF16),
                   jax.ShapeDtypeStruct((cfg.bp, 2, d), F32)],
        compiler_params=_cparams(("parallel", "parallel"), blk, 6 * _nbytes((cfg.lp, tc), F32)),
        name="sconv_prompt",
    )(proj, proj, proj, w)
    sblk = lambda off: pl.BlockSpec((cfg.tm, tc), lambda j: (cfg.sb, j + off * nb))
    own = pl.BlockSpec((cfg.tm, tc), lambda j: (0, j))
    blk = 4 * _nbytes((cfg.tm, tc), BF16) + 2 * _nbytes((cfg.tm, tc), F32)
    v_s, ch_s = pl.pallas_call(
        functools.partial(_sconv_sample_kernel, ds=cfg.ds),
        grid=(nb,),
        in_specs=[sblk(2), sblk(3), sblk(4), own, pl.BlockSpec((3, tc), lambda j: (0, j))],
        out_specs=[own, own],
        out_shape=[jax.ShapeDtypeStruct((cfg.tm, d), BF16),
                   jax.ShapeDtypeStruct((cfg.tm, d), F32)],
        compiler_params=_cparams(("parallel",), blk, 6 * _nbytes((cfg.tm, tc), F32)),
        name="sconv_sample",
    )(proj, proj, proj, e_sc, w)
    st_s = ch_s[:cfg.db * cfg.ds].reshape(cfg.db, cfg.ds, d)[:, cfg.ds - 2:]
    return v_p, v_s, st_p, st_s


def _tile_taps(x, halo, k):
    row8 = _rows((SUBLANES, x.shape[1]))
    taps = [x]
    for s in range(1, k):
        body = _shift_down(x, s)
        top = jnp.where(row8 >= s, body[:SUBLANES], _shift_down(halo, s))
        taps.append(jnp.concatenate([top, body[SUBLANES:]], axis=0))
    return taps


def _ffn_up_kernel(x_ref, wa_ref, wg_ref, ea_ref, eg_ref, cwa_ref, cwg_ref, cba_ref, cbg_ref, *refs, cfg, n_cast):
    h_ref, pst_ref, ups_ref = refs[n_cast:n_cast + 3]
    raw_ref, halo_ref = refs[-2:]
    _run_casts(refs[:n_cast], refs[n_cast + 3:-2])
    i = pl.program_id(0)
    j = pl.program_id(1)
    tm = cfg.tm

    @pl.when(jnp.logical_and(i == 0, j == 0))
    def _():
        raw_ref[...] = jnp.zeros_like(raw_ref)
        halo_ref[...] = jnp.zeros_like(halo_ref)

    def variant(sample):
        x = x_ref[...]
        raw_ref[0, SUBLANES:, :] = _dot(x, wa_ref[...])
        raw_ref[1, SUBLANES:, :] = _dot(x, wg_ref[...])
        lag = lambda half, s: raw_ref[half, pl.ds(SUBLANES - s, tm), :]
        if sample:
            t = _rows((tm, 1)) % cfg.ds
            taps = [[lag(half, 0)] + [jnp.where(t >= s, lag(half, s), e_ref[pl.ds(2 - s, tm), :]) for s in (1, 2)]
                    for half, e_ref in ((0, ea_ref), (1, eg_ref))]
            ups_ref[0] = taps[0][0]
            ups_ref[1] = taps[1][0]
        else:
            inside = i % cfg.nt != 0
            for half in (0, 1):
                raw_ref[half, :SUBLANES, :] = jnp.where(inside, halo_ref[j, half], 0.0)
            taps = [[lag(half, s) for s in range(3)] for half in (0, 1)]
        act = _conv_from_taps(taps[0], cwa_ref[...], cba_ref[...])
        gate = _conv_from_taps(taps[1], cwg_ref[...], cbg_ref[...])
        h_ref[...] = (_silu(act) * gate).astype(h_ref.dtype)
        for half in (0, 1):
            tail = raw_ref[half, tm:, :]
            pst_ref[0, half] = tail[SUBLANES - 2:]
            halo_ref[j, half] = tail

    pl.when(i < cfg.sb)(functools.partial(variant, False))
    pl.when(i == cfg.sb)(functools.partial(variant, True))


def _ffn_up(cfg, xn, w_up, e_fc, cw, cb, casts=()):
    d_ff, tm, sb, nt = cfg.d_ff, cfg.tm, cfg.sb, cfg.nt
    k = xn.shape[1]
    tn = _largest_tile(d_ff, 512)
    nb = d_ff // tn
    ni = cfg.r // tm
    only_sb = lambda i, j: jnp.where(i == sb, j, 0)
    cast_specs = [c.specs(ni * nb, lambda i, j: i * nb + j) for c in casts]
    blk = (_nbytes((tm, k), BF16) + 2 * _nbytes((k, tn), BF16) + 2 * _nbytes((tm, tn), F32)
           + _nbytes((tm, tn), BF16) + 2 * _nbytes((tm, tn), F32) + sum(s[3] for s in cast_specs))
    h, pst, ups, *cast_out = pl.pallas_call(
        functools.partial(_ffn_up_kernel, cfg=cfg, n_cast=len(casts)),
        grid=(ni, nb),
        in_specs=[pl.BlockSpec((tm, k), lambda i, j: (i, 0)),
                  pl.BlockSpec((k, tn), lambda i, j: (0, j)),
                  pl.BlockSpec((k, tn), lambda i, j: (0, j + nb)),
                  pl.BlockSpec((tm + SUBLANES, tn), lambda i, j: (0, only_sb(i, j))),
                  pl.BlockSpec((tm + SUBLANES, tn), lambda i, j: (0, only_sb(i, j) + nb)),
                  pl.BlockSpec((3, tn), lambda i, j: (0, j)), pl.BlockSpec((3, tn), lambda i, j: (0, j + nb)),
                  pl.BlockSpec((1, tn), lambda i, j: (0, j)), pl.BlockSpec((1, tn), lambda i, j: (0, j + nb))]
                 + [s[0] for s in cast_specs],
        out_specs=[pl.BlockSpec((tm, tn), lambda i, j: (i, j)),
                   pl.BlockSpec((1, 2, 2, tn), lambda i, j: (i, 0, 0, j)),
                   pl.BlockSpec((2, tm, tn), lambda i, j: (0, 0, only_sb(i, j)))] + [s[1] for s in cast_specs],
        out_shape=[jax.ShapeDtypeStruct((cfg.r, d_ff), BF16),
                   jax.ShapeDtypeStruct((ni, 2, 2, d_ff), F32),
                   jax.ShapeDtypeStruct((2, tm, d_ff), F32)] + [s[2] for s in cast_specs],
        scratch_shapes=[pltpu.VMEM((2, SUBLANES + tm, tn), F32), pltpu.VMEM((nb, 2, SUBLANES, tn), F32)],
        compiler_params=_cparams(("arbitrary", "arbitrary"), blk, 16 * _nbytes((tm, tn), F32)),
        name="ffn_up",
    )(xn, w_up, w_up, e_fc, e_fc, cw, cw, cb, cb, *[c.src for c in casts])
    last = pst[nt - 1:sb:nt]
    p_fc = jnp.concatenate([last[:, 0], last[:, 1]], axis=-1)
    return h, p_fc, ups, cast_out


def _ssd_gate_norm(y, z, gnw):
    y = y * _silu(z)
    return y * lax.rsqrt(jnp.mean(y * y, axis=-1, keepdims=True) + EPS) * gnw


def _ssd_prompt_kernel(x_ref, bm_ref, cm_ref, z_ref, dtr_ref, wx_ref, wb_ref, wc_ref, bx_ref, bb_ref, bc_ref,
                       dtb_ref, alog_ref, dsk_ref, gnw_ref, e_ref, tril_ref,
                       y_ref, hfin_ref, mcx_ref, mcb_ref, mcc_ref,
                       ht_ref, hx_ref, hb_ref, hc_ref, *, cfg):
    lp, hpg, n, gw = cfg.lp, cfg.hpg, cfg.n, cfg.gw
    c = SSD_CHUNK
    nch = 1 + cfg.seq // c
    ht_ref[...] = jnp.zeros_like(ht_ref)
    hx_ref[...] = jnp.zeros_like(hx_ref)
    hb_ref[...] = jnp.zeros_like(hb_ref)
    hc_ref[...] = jnp.zeros_like(hc_ref)
    a_neg = -jnp.exp(alog_ref[0])
    dtb = dtb_ref[0]
    e = e_ref[...]
    tril = tril_ref[...]
    li = lax.broadcasted_iota(jnp.int32, (c, c), 0)
    si = lax.broadcasted_iota(jnp.int32, (c, c), 1)
    causal = li >= si
    lane_lo = lax.broadcasted_iota(jnp.int32, (c, 2 * HEAD_DIM), 1) < HEAD_DIM

    def conv_silu(src_ref, nxt, buf_ref, w_ref, b_ref, first):
        y = _silu(_conv_from_taps([buf_ref[pl.ds(SUBLANES - s, c), :] for s in range(4)], w_ref[...], b_ref[...]))
        buf_ref[:SUBLANES, :] = jnp.where(first, buf_ref[N_META:N_META + SUBLANES, :], buf_ref[c:, :])
        buf_ref[SUBLANES:, :] = src_ref[nxt, :].astype(F32)
        return y

    for src_ref, buf_ref in ((x_ref, hx_ref), (bm_ref, hb_ref), (cm_ref, hc_ref)):
        buf_ref[SUBLANES:, :] = src_ref[:c, :].astype(F32)

    def chunk(k, carry):
        first = k == 0
        start = pl.multiple_of(jnp.where(first, 0, N_META + c * (k - 1)), BF16_ROWS)
        rows = pl.ds(start, c)
        nxt = pl.ds(pl.multiple_of(jnp.minimum(N_META + c * k, lp - c), BF16_ROWS), c)
        xh = conv_silu(x_ref, nxt, hx_ref, wx_ref, bx_ref, first)
        bc = conv_silu(bm_ref, nxt, hb_ref, wb_ref, bb_ref, first)
        cc = conv_silu(cm_ref, nxt, hc_ref, wc_ref, bc_ref, first)
        live = jnp.logical_or(jnp.logical_not(first), _rows((c, LANES)) < N_META)
        dt = jnp.where(live, _softplus(_group_lanes(dtr_ref[rows, :], pl.program_id(1), hpg) + dtb), 0.0)
        a = dt * a_neg
        a1 = a.astype(BF16)
        r1 = a - a1.astype(F32)
        a2 = r1.astype(BF16)
        a3 = (r1 - a2.astype(F32)).astype(BF16)
        cs = _dot(tril, a1) + _dot(tril, a2) + _dot(tril, a3)
        cs_t = cs.T
        cs_last = cs[c - 1:c]
        ecs = jnp.exp(cs)
        both_e = _dot(jnp.concatenate([dt, dt * jnp.exp(cs_last - cs)], axis=0).astype(BF16), e)
        xs_b = (xh * both_e[:c]).astype(BF16)
        xsw_b = (xh * both_e[c:]).astype(BF16)
        dec_row = _expand_heads(jnp.broadcast_to(ecs[c - 1:c], (SUBLANES, LANES)), e)[:1]
        scores = jnp.where(causal, _dot_nt(cc.astype(BF16), bc.astype(BF16)), 0.0)
        ht = ht_ref[...]
        ht_b = ht.astype(BF16)
        pairs = []
        for q in range(hpg // 2):
            cols = slice(2 * HEAD_DIM * q, 2 * HEAD_DIM * (q + 1))
            rhs = jnp.concatenate([xs_b[:, cols], ht_b[:, cols]], axis=0)
            res = []
            for r in (2 * q, 2 * q + 1):
                m = scores * jnp.exp(jnp.minimum(cs[:, r:r + 1] - cs_t[r:r + 1, :], 0.0))
                lhs = jnp.concatenate([m, cc * ecs[:, r:r + 1]], axis=1).astype(BF16)
                res.append(_dot(lhs, rhs))
            pairs.append(jnp.where(lane_lo, res[0], res[1]))
        y = jnp.concatenate(pairs, axis=1) if len(pairs) > 1 else pairs[0]
        ht_ref[...] = ht * dec_row + _dot(bc.T.astype(BF16), xsw_b)
        y = y + dsk_ref[...] * xh
        y_ref[rows, :] = _ssd_gate_norm(y, z_ref[rows, :].astype(F32), gnw_ref[...]).astype(y_ref.dtype)
        return carry

    lax.fori_loop(0, nch, chunk, 0)
    hfin_ref[0] = ht_ref[...].T.reshape(hpg, HEAD_DIM, n)
    tail = lambda ref: ref[lp - BF16_ROWS:lp, :].astype(F32)[BF16_ROWS - 3:]
    mcx_ref[0] = tail(x_ref)
    mcb_ref[0] = tail(bm_ref)
    mcc_ref[0] = tail(cm_ref)


def _ssd_prompt(cfg, proj, dtr_g, prm):
    lp, gw, n, hpg, g_ = cfg.lp, cfg.gw, cfg.n, cfg.hpg, N_GROUPS
    xb0 = 7 * cfg.d // gw
    zb0 = 5 * cfg.d // gw
    bb0 = 9 * cfg.d // n
    cvb = cfg.d_inner // n
    colblk = lambda w, f: pl.BlockSpec((lp, w), f)
    par = lambda rows, w, f: pl.BlockSpec((rows, w), f)
    vec3 = pl.BlockSpec((1, 1, LANES), lambda b, g: (g, 0, 0))
    in_specs = [
        colblk(gw, lambda b, g: (b, xb0 + g)), colblk(n, lambda b, g: (b, bb0 + g)),
        colblk(n, lambda b, g: (b, bb0 + g_ + g)), colblk(gw, lambda b, g: (b, zb0 + g)),
        colblk(LANES, lambda b, g: (b, 0)),
        par(4, gw, lambda b, g: (0, g)), par(4, n, lambda b, g: (0, cvb + g)), par(4, n, lambda b, g: (0, cvb + g_ + g)),
        par(1, gw, lambda b, g: (0, g)), par(1, n, lambda b, g: (0, cvb + g)), par(1, n, lambda b, g: (0, cvb + g_ + g)),
        vec3, vec3,
        par(1, gw, lambda b, g: (0, g)), par(1, gw, lambda b, g: (0, g)),
        par(LANES, gw, lambda b, g: (0, 0)), par(SSD_CHUNK, SSD_CHUNK, lambda b, g: (0, 0)),
    ]
    out_specs = [
        pl.BlockSpec((lp, gw), lambda b, g: (b, g)),
        pl.BlockSpec((1, hpg, HEAD_DIM, n), lambda b, g: (b, g, 0, 0)),
        pl.BlockSpec((1, 3, gw), lambda b, g: (b, 0, g)),
        pl.BlockSpec((1, 3, n), lambda b, g: (b, 0, g)),
        pl.BlockSpec((1, 3, n), lambda b, g: (b, 0, g)),
    ]
    out_shape = [
        jax.ShapeDtypeStruct((cfg.bp * cfg.lp, cfg.d_inner), BF16),
        jax.ShapeDtypeStruct((cfg.bp, cfg.h, HEAD_DIM, n), F32),
        jax.ShapeDtypeStruct((cfg.bp, 3, cfg.d_inner), F32),
        jax.ShapeDtypeStruct((cfg.bp, 3, g_ * n), F32),
        jax.ShapeDtypeStruct((cfg.bp, 3, g_ * n), F32),
    ]
    blk = 3 * _nbytes((lp, gw), BF16) + 2 * _nbytes((lp, n), BF16) + _nbytes((lp, LANES), F32)
    y, hfin, mcx, mcb, mcc = pl.pallas_call(
        functools.partial(_ssd_prompt_kernel, cfg=cfg),
        grid=(cfg.bp, g_),
        in_specs=in_specs, out_specs=out_specs, out_shape=out_shape,
        scratch_shapes=[pltpu.VMEM((n, gw), F32), pltpu.VMEM((SUBLANES + SSD_CHUNK, gw), F32),
                        pltpu.VMEM((SUBLANES + SSD_CHUNK, n), F32), pltpu.VMEM((SUBLANES + SSD_CHUNK, n), F32)],
        compiler_params=_cparams(("parallel", "parallel"), blk, 24 * _nbytes((SSD_CHUNK, gw), F32) + (4 << 20)),
        name="ssd_prompt",
    )(proj, proj, proj, proj, dtr_g,
      prm["mconv_w"], prm["mconv_w"], prm["mconv_w"], prm["mconv_b"], prm["mconv_b"], prm["mconv_b"],
      prm["dt_bias_g"], prm["a_log_g"], prm["d_skip_e"], prm["gnorm_w"], prm["expand"], prm["tril"])
    return y, hfin, jnp.concatenate([mcx, mcb, mcc], axis=-1)


def _ssd_tok_kernel(x_ref, bm_ref, cm_ref, dtr_ref, ex_ref, eb_ref, ec_ref, wx_ref, wb_ref, wc_ref,
                    bx_ref, bb_ref, bc_ref, dtb_ref, alog_ref, dsk_ref, e_ref,
                    ypre_ref, ecs_ref, xsw_ref, bo_ref, co_ref, dec_ref, *, ds, hpg):
    e = e_ref[...]
    conv = lambda ref, eref, w, b: _silu(_conv_from_taps(
        _sample_taps(ref[...].astype(F32), eref[...], 4, ds), w[...], b[...]))
    xh = conv(x_ref, ex_ref, wx_ref, bx_ref)
    bc = conv(bm_ref, eb_ref, wb_ref, bb_ref)
    cc = conv(cm_ref, ec_ref, wc_ref, bc_ref)
    dt = _softplus(_group_lanes(dtr_ref[...], pl.program_id(0), hpg) + dtb_ref[0])
    a = dt * (-jnp.exp(alog_ref[0]))
    t = _rows(a.shape) % ds
    t1 = _rows((a.shape[0], 1)) % ds
    cs = a
    for j in range(1, ds):
        cs = cs + jnp.where(t >= j, _shift_down(a, j), 0.0)
    cs_last = jnp.zeros_like(cs)
    for j in range(ds):
        cs_last = cs_last + jnp.where(t == ds - 1 - j, _shift_up(cs, j), 0.0)
    ecs_e = _expand_heads(jnp.exp(cs), e)
    xs = xh * _expand_heads(dt, e)
    y = dsk_ref[...] * xh
    for j in range(ds):
        sc = jnp.sum(cc * _shift_down(bc, j), axis=-1, keepdims=True)
        diff = jnp.where(t >= j, cs - _shift_down(cs, j), 0.0)
        coef = jnp.where(t1 >= j, sc, 0.0) * jnp.exp(diff)
        y = y + _expand_heads(coef, e) * _shift_down(xs, j)
    ypre_ref[...] = y
    ecs_ref[...] = ecs_e
    xsw_ref[...] = xs * _expand_heads(jnp.exp(cs_last - cs), e)
    bo_ref[...] = bc
    co_ref[...] = cc
    dec_ref[...] = jnp.exp(cs_last)


def _ssd_state_kernel(dec_ref, h0_ref, c_ref, b_ref, xsw_ref, *refs, cfg):
    hout_ref, yoff_ref = refs[-2:]
    hpg, n, gw, ds = cfg.hpg, cfg.n, cfg.gw, cfg.ds
    i = pl.program_id(0)
    nseq = SUBLANES // ds
    pad = jnp.zeros((LANES - SUBLANES, n), F32)
    row8 = _rows((SUBLANES, 1)) // ds
    cmat = c_ref[...]
    bmat = b_ref[...]
    xsw = xsw_ref[...]
    youts = []
    for g in range(N_GROUPS):
        cg = cmat[:, g * n:(g + 1) * n].astype(BF16)
        bg = bmat[:, g * n:(g + 1) * n]
        xw = jnp.concatenate([xsw[:, g * gw:(g + 1) * gw], jnp.zeros((LANES - SUBLANES, gw), F32)], axis=0)
        xw_t = xw.T.astype(BF16)
        yg = jnp.zeros((SUBLANES, gw), F32)
        for s in range(nseq):
            h0 = h0_ref[s, g * hpg:(g + 1) * hpg].reshape(gw, n)
            yg = jnp.where(row8 == s, _dot_nt(cg, h0.astype(BF16)), yg)
            bs = jnp.concatenate([jnp.where(row8 == s, bg, 0.0), pad], axis=0).astype(BF16)
            dh = _dot(xw_t, bs)
            for r in range(hpg):
                sl = slice(r * HEAD_DIM, (r + 1) * HEAD_DIM)
                hout_ref[s, g * hpg + r] = h0[sl] * dec_ref[i * nseq + s, g * hpg + r] + dh[sl]
        youts.append(yg)
    yoff_ref[...] = jnp.concatenate(youts, axis=1)


def _ssd_post_kernel(ypre_ref, yoff_ref, ecs_ref, z_ref, gnw_ref, o_ref):
    y = ypre_ref[...] + yoff_ref[...] * ecs_ref[...]
    o_ref[...] = _ssd_gate_norm(y, z_ref[...].astype(F32), gnw_ref[...]).astype(o_ref.dtype)


def _ssd_sample(cfg, proj, dtr_g, e_mc, h_all, layer, h_new, prm):
    tm, gw, n, g_, sb = cfg.tm, cfg.gw, cfg.n, N_GROUPS, cfg.sb
    xb0 = 7 * cfg.d // gw
    zb0 = 5 * cfg.d // gw
    bb0 = 9 * cfg.d // n
    cvb = cfg.d_inner // n
    row = lambda w, f: pl.BlockSpec((tm, w), f)
    par = lambda rows, w, f: pl.BlockSpec((rows, w), f)
    vec3 = pl.BlockSpec((1, 1, LANES), lambda g: (g, 0, 0))
    in_specs = [
        row(gw, lambda g: (sb, xb0 + g)), row(n, lambda g: (sb, bb0 + g)), row(n, lambda g: (sb, bb0 + g_ + g)),
        row(LANES, lambda g: (sb, 0)),
        row(gw, lambda g: (0, g)), row(n, lambda g: (0, cvb + g)), row(n, lambda g: (0, cvb + g_ + g)),
        par(4, gw, lambda g: (0, g)), par(4, n, lambda g: (0, cvb + g)), par(4, n, lambda g: (0, cvb + g_ + g)),
        par(1, gw, lambda g: (0, g)), par(1, n, lambda g: (0, cvb + g)), par(1, n, lambda g: (0, cvb + g_ + g)),
        vec3, vec3, par(1, gw, lambda g: (0, g)), par(LANES, gw, lambda g: (0, 0)),
    ]
    own = lambda w: pl.BlockSpec((tm, w), lambda g: (0, g))
    blk = 2 * _nbytes((tm, gw), BF16) + 5 * _nbytes((tm, gw), F32)
    ypre, ecs, xsw, bo, co, dec = pl.pallas_call(
        functools.partial(_ssd_tok_kernel, ds=cfg.ds, hpg=cfg.hpg),
        grid=(g_,),
        in_specs=in_specs,
        out_specs=[own(gw), own(gw), own(gw), own(n), own(n), own(LANES)],
        out_shape=[jax.ShapeDtypeStruct((tm, cfg.d_inner), F32)] * 3
                  + [jax.ShapeDtypeStruct((tm, g_ * n), F32)] * 2
                  + [jax.ShapeDtypeStruct((tm, g_ * LANES), F32)],
        compiler_params=_cparams(("parallel",), blk, 16 * _nbytes((tm, gw), F32)),
        name="ssd_tok",
    )(proj, proj, proj, dtr_g, e_mc, e_mc, e_mc,
      prm["mconv_w"], prm["mconv_w"], prm["mconv_w"], prm["mconv_b"], prm["mconv_b"], prm["mconv_b"],
      prm["dt_bias_g"], prm["a_log_g"], prm["d_skip_e"], prm["expand"])
    nrow = cfg.db * cfg.ds
    dec_sh = dec[:nrow].reshape(cfg.db, cfg.ds, g_, LANES)[:, cfg.ds - 1, :, :cfg.hpg].reshape(cfg.db, cfg.h)
    nseq = SUBLANES // cfg.ds
    rows8 = lambda w: pl.BlockSpec((SUBLANES, w), lambda i: (i, 0))
    hblk = pl.BlockSpec((None, nseq, cfg.h, HEAD_DIM, n), lambda i: (layer, i, 0, 0, 0))
    blk = 2 * _nbytes((nseq, cfg.h, HEAD_DIM, n), F32) + 3 * _nbytes((SUBLANES, cfg.d_inner), F32)
    hbm = pl.BlockSpec(memory_space=pl.ANY)
    aliased = [jnp.zeros((tm, cfg.d_inner), F32)] + ([] if h_new is None else [h_new])
    hout, yoff = pl.pallas_call(
        functools.partial(_ssd_state_kernel, cfg=cfg),
        grid=(cfg.db // nseq,),
        in_specs=[pl.BlockSpec(memory_space=pltpu.SMEM), hblk, rows8(g_ * n), rows8(g_ * n), rows8(cfg.d_inner)]
                 + [hbm] * len(aliased),
        out_specs=[hblk, rows8(cfg.d_inner)],
        out_shape=[jax.ShapeDtypeStruct(h_all.shape, F32), jax.ShapeDtypeStruct((tm, cfg.d_inner), F32)],
        input_output_aliases={5: 1} if h_new is None else {5: 1, 6: 0},
        compiler_params=_cparams(("arbitrary",), blk, 8 << 20),
        name="ssd_state",
    )(dec_sh, h_all, co, bo, xsw, *aliased)
    blk = 3 * _nbytes((tm, gw), F32) + 2 * _nbytes((tm, gw), BF16)
    yg = pl.pallas_call(
        _ssd_post_kernel,
        grid=(g_,),
        in_specs=[own(gw), own(gw), own(gw), row(gw, lambda g: (sb, zb0 + g)), par(1, gw, lambda g: (0, g))],
        out_specs=own(gw),
        out_shape=jax.ShapeDtypeStruct((tm, cfg.d_inner), BF16),
        compiler_params=_cparams(("parallel",), blk, 4 * _nbytes((tm, gw), F32)),
        name="ssd_post",
    )(ypre, yoff, ecs, proj, prm["gnorm_w"])
    return yg, hout


def _flat_state(cfg, buf):
    db, km1, c = buf.shape
    nseq = (cfg.tm + SUBLANES) // cfg.ds
    assert nseq * cfg.ds == cfg.tm + SUBLANES
    return jnp.pad(buf.astype(F32), ((0, nseq - db), (0, cfg.ds - km1), (0, 0))).reshape(nseq * cfg.ds, c)


def _dt_weights(cfg, w_in):
    assert cfg.h <= LANES
    return jnp.pad(w_in[:, :, cfg.nmain:], ((0, 0), (0, 0), (0, LANES - cfg.h))).astype(BF16)


def _layer_params(cfg, sconv_w, mconv_w, mconv_b, dt_bias, a_log, d_skip, gnorm_w, fconv_w, fconv_b):
    g_, hpg = N_GROUPS, cfg.hpg
    grp = lambda v: jnp.pad(v.astype(F32).reshape(g_, 1, hpg), ((0, 0), (0, 0), (0, LANES - hpg)))
    head = jnp.arange(LANES)[:, None]
    col = jnp.arange(cfg.gw)[None, :] // HEAD_DIM
    return dict(
        sconv_w=sconv_w.astype(F32),
        mconv_w=mconv_w.astype(F32), mconv_b=mconv_b.astype(F32).reshape(1, -1),
        dt_bias_g=grp(dt_bias), a_log_g=grp(a_log),
        d_skip_e=jnp.repeat(d_skip.astype(F32), HEAD_DIM).reshape(1, -1),
        gnorm_w=gnorm_w.astype(F32).reshape(1, -1),
        fconv_w=fconv_w.astype(F32), fconv_b=fconv_b.astype(F32).reshape(1, -1),
        expand=((head == col) & (head < hpg)).astype(BF16),
        tril=(jnp.arange(SSD_CHUNK)[:, None] >= jnp.arange(SSD_CHUNK)[None, :]).astype(BF16),
    )


def _layer(cfg, x, xn, w_in_b, w_dt_b, f32_mats, layer, prm, st_sc, st_mc, h_all, h_new, st_fc):
    d, d_ff = cfg.d, cfg.d_ff
    nrow = cfg.db * cfg.ds
    srows = slice(cfg.sb * cfg.tm, cfg.sb * cfg.tm + nrow)
    own = ("w_up", "w_down", "wa_out", "wb_out", "w_o")
    proj, *cast = _mm(cfg, xn, w_in_b, cfg.nmain, BF16, "proj", casts=[Cast(f32_mats[n], layer) for n in own])
    mats = dict(zip(own, cast))
    dtr_g = _mm(cfg, xn, w_dt_b, LANES, F32, "proj_dt")
    v, v_s, p_sc, s_sc = _sconv(cfg, proj, _flat_state(cfg, st_sc), prm["sconv_w"])
    out_a = _mm(cfg, v, mats["wa_out"], d, BF16, "wa_out", x_tail=v_s)
    yg, p_h, p_mc = _ssd_prompt(cfg, proj, dtr_g, prm)
    yg_s, h_new = _ssd_sample(cfg, proj, dtr_g, _flat_state(cfg, st_mc), h_all, layer, h_new, prm)
    s_mc = proj[srows, 7 * d:7 * d + cfg.conv_dim].astype(F32).reshape(cfg.db, cfg.ds, cfg.conv_dim)[:, cfg.ds - 3:]
    mix = _mm(cfg, yg, mats["wb_out"], d, BF16, "wb_out", epilogue="mix", x_tail=yg_s,
              extras=((proj, 0, BF16), (proj, 1, BF16), (out_a, 0, BF16)))
    x = _mm(cfg, mix, mats["w_o"], d, F32, "w_o", epilogue="resid", extras=((x, 0, F32),))
    xn2 = _rmsnorm(cfg, x, prm["norm2_g"], BF16)
    more = layer + 1 < f32_mats["w_in"].shape[0]
    hmid, p_fc, ups, nxt = _ffn_up(cfg, xn2, mats["w_up"], _flat_state(cfg, st_fc), prm["fconv_w"], prm["fconv_b"],
                                   casts=[Cast(f32_mats["w_in"], layer + 1)] if more else [])
    s_fc = jnp.concatenate([ups[0, :nrow], ups[1, :nrow]], axis=-1).reshape(cfg.db, cfg.ds, 2 * d_ff)[:, cfg.ds - 2:]
    x = _mm(cfg, hmid, mats["w_down"], d, F32, "w_down", epilogue="resid", extras=((x, 0, F32),))
    return x, h_new, (p_sc, p_mc, p_h, p_fc, s_sc, s_mc, s_fc), (nxt[0] if more else None)


def kernel(x_prompt, x_sample, state_sconv, state_mconv, state_ssm, state_fconv, meta_tokens, norm1_g, w_in,
           sconv_w, wa_out, mconv_w, mconv_b, dt_bias, a_log, d_skip, gnorm_w, wb_out, w_o, norm2_g, w_up,
           fconv_w, fconv_b, w_down, final_g):
    cfg = _make_cfg(x_prompt, x_sample, state_ssm, w_up)
    depth = w_in.shape[0]
    dtype = x_prompt.dtype
    nrow = cfg.db * cfg.ds
    meta = meta_tokens.astype(dtype)
    x = jnp.concatenate([piece for b in range(cfg.bp) for piece in (meta, x_prompt[b])]
                        + [x_sample.reshape(nrow, cfg.d),
                           jnp.zeros((cfg.r - cfg.bp * cfg.lp - nrow, cfg.d), dtype)], axis=0).astype(F32)
    f32_mats = dict(w_in=w_in, wa_out=wa_out, wb_out=wb_out, w_o=w_o, w_up=w_up, w_down=w_down)
    w_dt_b = _dt_weights(cfg, w_in)
    w_in_b = _cast_layer(w_in, 0)
    states = []
    h_new = None
    for l in range(depth):
        prm = _layer_params(cfg, sconv_w[l], mconv_w[l], mconv_b[l], dt_bias[l], a_log[l], d_skip[l], gnorm_w[l],
                            fconv_w[l], fconv_b[l])
        prm["norm2_g"] = norm2_g[l]
        xn = _rmsnorm(cfg, x, norm1_g[l], BF16)
        x, h_new, st, w_in_b = _layer(cfg, x, xn, w_in_b, w_dt_b[l], f32_mats, l, prm, state_sconv[l],
                                      state_mconv[l], state_ssm, h_new, state_fconv[l])
        states.append(st)
    y_prompt = _rmsnorm_prompt_tokens(cfg, x, final_g).astype(dtype)
    y_s = _rmsnorm(cfg, x, final_g, F32, cfg.sb * cfg.tm, cfg.tm)
    y_sample = y_s[:nrow].reshape(cfg.db, cfg.ds, cfg.d).astype(dtype)
    p_sc, p_mc, p_h, p_fc, s_sc, s_mc, s_fc = [jnp.stack([s[i] for s in states]).astype(dtype) for i in range(7)]
    return (y_prompt, y_sample, p_sc, p_mc, p_h, p_fc, s_sc, s_mc, h_new.astype(dtype), s_fc)
```

```python
import functools
import math
from typing import NamedTuple

import jax
import jax.numpy as jnp
from jax import lax
from jax.experimental import pallas as pl
from jax.experimental.pallas import tpu as pltpu

F32 = jnp.float32
BF16 = jnp.bfloat16

N_META = 16
N_GROUPS = 8
HEAD_DIM = 64
SSD_CHUNK = 128
EPS = 1e-6

LANES = 128
SUBLANES = 8
BF16_ROWS = 16
VMEM_PHYSICAL_BYTES = 64 * 1024 * 1024
VMEM_BUDGET_BYTES = 56 * 1024 * 1024
COMPILER_SCRATCH_BYTES = 6 * 1024 * 1024
NORM_ROWS = 256
CAST_CHUNK_BYTES = 8 * 1024 * 1024
MM_VMEM_BYTES = 48 * 1024 * 1024


class Cfg(NamedTuple):
    d: int
    bp: int
    seq: int
    lp: int
    db: int
    ds: int
    h: int
    hpg: int
    gw: int
    n: int
    d_inner: int
    conv_dim: int
    d_ff: int
    nt: int
    tm: int
    sb: int
    r: int
    nmain: int


def _largest_tile(dim, target, quantum=LANES):
    best = None
    t = quantum
    while t <= min(dim, target):
        if dim % t == 0:
            best = t
        t += quantum
    assert best is not None, (dim, target, quantum)
    return best


def _make_cfg(x_prompt, x_sample, state_ssm, w_up):
    bp, seq, d = x_prompt.shape
    db, ds, _ = x_sample.shape
    _, _, h, p, n = state_ssm.shape
    assert p == HEAD_DIM and h % N_GROUPS == 0 and seq % SSD_CHUNK == 0
    hpg = h // N_GROUPS
    assert hpg % 2 == 0 and hpg <= LANES
    d_inner = h * HEAD_DIM
    lp = N_META + seq
    d_ff = w_up.shape[-1] // 2
    nt = 3 if (lp % (3 * BF16_ROWS) == 0 and lp // 3 >= db * ds) else 1
    tm = lp // nt
    assert tm % BF16_ROWS == 0 and db * ds <= tm and ds >= 3
    sb = bp * nt
    return Cfg(d=d, bp=bp, seq=seq, lp=lp, db=db, ds=ds, h=h, hpg=hpg, gw=hpg * HEAD_DIM, n=n,
               d_inner=d_inner, conv_dim=d_inner + 2 * N_GROUPS * n, d_ff=d_ff, nt=nt, tm=tm,
               sb=sb, r=(sb + 1) * tm, nmain=5 * d + d_inner + d_inner + 2 * N_GROUPS * n)


def _cparams(semantics, block_bytes, temp_bytes=0):
    need = 2 * block_bytes + temp_bytes + COMPILER_SCRATCH_BYTES
    return pltpu.CompilerParams(dimension_semantics=semantics,
                                vmem_limit_bytes=int(min(max(need, 16 << 20), VMEM_BUDGET_BYTES)))


def _nbytes(shape, dtype):
    return math.prod(shape) * jnp.dtype(dtype).itemsize


def _silu(x):
    half = 0.5 * x
    return half + half * jnp.tanh(half)


def _softplus(x):
    return jnp.maximum(x, 0.0) + jnp.log1p(jnp.exp(-jnp.abs(x)))


def _shift_down(x, s):
    return x if s == 0 else pltpu.roll(x, s, axis=0)


def _shift_up(x, s):
    return x if s == 0 else pltpu.roll(x, x.shape[0] - s, axis=0)


def _rows(shape):
    return lax.broadcasted_iota(jnp.int32, shape, 0)


def _dot(a, b):
    return jnp.dot(a, b, preferred_element_type=F32)


def _dot_nt(a, b):
    return lax.dot_general(a, b, (((1,), (1,)), ((), ())), preferred_element_type=F32)


def _group_lanes(v, g, hpg):
    return pltpu.roll(v, (LANES - hpg * g) % LANES, axis=1)


def _expand_heads(v, e):
    hi = v.astype(BF16)
    lo = (v - hi.astype(F32)).astype(BF16)
    return _dot(hi, e) + _dot(lo, e)


def _rmsnorm_kernel(x_ref, g_ref, o_ref):
    x = x_ref[...]
    y = x * lax.rsqrt(jnp.mean(x * x, axis=-1, keepdims=True) + EPS)
    o_ref[...] = (y * g_ref[...]).astype(o_ref.dtype)


def _rmsnorm(cfg, x, g, out_dtype, row0=0, nrows=None):
    nrows = cfg.r if nrows is None else nrows
    quantum = BF16_ROWS if jnp.dtype(out_dtype).itemsize < 4 else SUBLANES
    tr = _largest_tile(math.gcd(nrows, row0) if row0 else nrows, NORM_ROWS if row0 == 0 else 2 * NORM_ROWS, quantum)
    blk = _nbytes((tr, cfg.d), F32) + _nbytes((tr, cfg.d), out_dtype)
    b0 = row0 // tr
    return pl.pallas_call(
        _rmsnorm_kernel,
        grid=(nrows // tr,),
        in_specs=[pl.BlockSpec((tr, cfg.d), lambda i: (i + b0, 0)),
                  pl.BlockSpec((1, cfg.d), lambda i: (0, 0))],
        out_specs=pl.BlockSpec((tr, cfg.d), lambda i: (i, 0)),
        out_shape=jax.ShapeDtypeStruct((nrows, cfg.d), out_dtype),
        compiler_params=_cparams(("parallel",), blk, _nbytes((tr, cfg.d), F32)),
        name="rmsnorm",
    )(x, g.reshape(1, cfg.d))


def _rmsnorm_prompt_tokens(cfg, x, g):
    tr = _largest_tile(cfg.seq, NORM_ROWS, SUBLANES)
    blk = 2 * _nbytes((tr, cfg.d), F32)
    return pl.pallas_call(
        _rmsnorm_kernel,
        grid=(cfg.bp, cfg.seq // tr),
        in_specs=[pl.BlockSpec((pl.Element(tr), pl.Element(cfg.d)),
                               lambda b, t: (pl.multiple_of(b * cfg.lp + N_META + t * tr, SUBLANES), 0)),
                  pl.BlockSpec((1, cfg.d), lambda b, t: (0, 0))],
        out_specs=pl.BlockSpec((None, tr, cfg.d), lambda b, t: (b, t, 0)),
        out_shape=jax.ShapeDtypeStruct((cfg.bp, cfg.seq, cfg.d), F32),
        compiler_params=_cparams(("parallel", "parallel"), blk, _nbytes((tr, cfg.d), F32)),
        name="rmsnorm_out",
    )(x, g.reshape(1, cfg.d))


def _cast_kernel(s_ref, d_ref):
    d_ref[...] = s_ref[...].astype(d_ref.dtype)


def _cast_layer(w, layer):
    _, rows, cols = w.shape
    r = _largest_tile(rows, max(BF16_ROWS, CAST_CHUNK_BYTES // (cols * 6)), BF16_ROWS)
    return pl.pallas_call(
        _cast_kernel,
        grid=(rows // r,),
        in_specs=[pl.BlockSpec((None, r, cols), lambda i: (layer, i, 0))],
        out_specs=pl.BlockSpec((r, cols), lambda i: (i, 0)),
        out_shape=jax.ShapeDtypeStruct((rows, cols), BF16),
        compiler_params=_cparams(("parallel",), _nbytes((r, cols), F32) + _nbytes((r, cols), BF16)),
        name="cast_bf16",
    )(w)


class Cast(NamedTuple):
    src: jax.Array
    layer: int

    def chunk_rows(self, nsteps):
        rows = self.src.shape[1]
        return next(r for r in range(BF16_ROWS, rows + 1, BF16_ROWS) if rows % r == 0 and rows // r <= nsteps)

    def specs(self, nsteps, step_of):
        _, rows, cols = self.src.shape
        r = self.chunk_rows(nsteps)
        last = rows // r - 1
        idx = lambda *g: jnp.minimum(step_of(*g), last)
        return (pl.BlockSpec((None, r, cols), lambda *g: (self.layer, idx(*g), 0)),
                pl.BlockSpec((r, cols), lambda *g: (idx(*g), 0)),
                jax.ShapeDtypeStruct((rows, cols), BF16),
                _nbytes((r, cols), F32) + _nbytes((r, cols), BF16))


def _run_casts(src_refs, dst_refs):
    for s, d in zip(src_refs, dst_refs):
        d[...] = s[...].astype(d.dtype)


def _mm_kernel(x_ref, w_ref, *refs, epilogue, n_extra, n_cast, has_tail):
    if has_tail:
        xt_ref, refs = refs[0], refs[1:]
    o_ref = refs[n_extra + n_cast]
    _run_casts(refs[n_extra:n_extra + n_cast], refs[n_extra + n_cast + 1:])

    def product(lhs_ref):
        acc = _dot(lhs_ref[...], w_ref[...])
        if epilogue == "plain":
            o_ref[...] = acc.astype(o_ref.dtype)
        elif epilogue == "resid":
            o_ref[...] = refs[0][...] + acc
        elif epilogue == "mix":
            ga_ref, gb_ref, a_ref = refs[:3]
            ga = jax.nn.sigmoid(ga_ref[...].astype(F32))
            gb = jax.nn.sigmoid(gb_ref[...].astype(F32))
            o_ref[...] = (ga * a_ref[...].astype(F32) + gb * acc).astype(o_ref.dtype)
        else:
            raise ValueError(epilogue)

    if has_tail:
        last = pl.num_programs(0) - 1
        pl.when(pl.program_id(0) < last)(functools.partial(product, x_ref))
        pl.when(pl.program_id(0) == last)(functools.partial(product, xt_ref))
    else:
        product(x_ref)


def _mm(cfg, x, w, ncols, out_dtype, name, epilogue="plain", extras=(), casts=(), x_tail=None):
    k = x.shape[1]
    per_elem = jnp.dtype(out_dtype).itemsize + sum(jnp.dtype(dt).itemsize for _, _, dt in extras)
    ni = cfg.r // cfg.tm
    tails = [] if x_tail is None else [x_tail]
    xrow = (lambda i: i) if x_tail is None else (lambda i: jnp.minimum(i, ni - 2))

    def plan(tn):
        nb = ncols // tn
        cast_specs = [c.specs(ni * nb, lambda i, j: i * nb + j) for c in casts]
        blocks = (_nbytes((cfg.tm, k), BF16) + _nbytes((k, tn), BF16) + cfg.tm * tn * per_elem
                  + sum(s[3] for s in cast_specs))
        return nb, cast_specs, blocks

    single = len(tails) * _nbytes((cfg.tm, k), BF16)
    tn = next(t for t in (1024, 512, 256, LANES)
              if ncols % t == 0 and 2 * plan(t)[2] + single + 2 * _nbytes((cfg.tm, t), F32) <= MM_VMEM_BYTES)
    nb, cast_specs, blocks = plan(tn)
    tile = lambda off: pl.BlockSpec((cfg.tm, tn), lambda i, j: (i, j + off * nb))
    outs = pl.pallas_call(
        functools.partial(_mm_kernel, epilogue=epilogue, n_extra=len(extras), n_cast=len(casts),
                          has_tail=bool(tails)),
        grid=(ni, nb),
        in_specs=[pl.BlockSpec((cfg.tm, k), lambda i, j: (xrow(i), 0)),
                  pl.BlockSpec((k, tn), lambda i, j: (0, j))]
                 + [pl.BlockSpec((cfg.tm, k), lambda i, j: (0, 0), pipeline_mode=pl.Buffered(1)) for _ in tails]
                 + [tile(off) for _, off, _ in extras] + [s[0] for s in cast_specs],
        out_specs=[tile(0)] + [s[1] for s in cast_specs],
        out_shape=[jax.ShapeDtypeStruct((cfg.r, ncols), out_dtype)] + [s[2] for s in cast_specs],
        compiler_params=_cparams(("arbitrary", "arbitrary"), blocks, single + 2 * _nbytes((cfg.tm, tn), F32)),
        name=name,
    )(x, w, *tails, *[a for a, _, _ in extras], *[c.src for c in casts])
    return outs if casts else outs[0]


def _prompt_taps(x, k):
    row = _rows(x.shape)
    return [x] + [jnp.where(row >= s, _shift_down(x, s), 0.0) for s in range(1, k)]


def _sample_taps(x, e, k, ds):
    t = _rows(x.shape) % ds
    return [x] + [jnp.where(t >= s, _shift_down(x, s), _shift_up(e, k - 1 - s)) for s in range(1, k)]


def _conv_from_taps(taps, w, b=None):
    k = len(taps)
    y = taps[0] * w[k - 1:k]
    for s in range(1, k):
        y = y + taps[s] * w[k - 1 - s:k - s]
    return y if b is None else y + b


def _sconv_prompt_kernel(h_ref, b_ref, c_ref, w_ref, v_ref, st_ref, *, lp):
    ch = c_ref[...].astype(F32) * h_ref[...].astype(F32)
    u = _conv_from_taps(_prompt_taps(ch, 3), w_ref[...])
    v_ref[...] = (b_ref[...].astype(F32) * u).astype(v_ref.dtype)
    st_ref[0] = ch[lp - SUBLANES:lp][SUBLANES - 2:]


def _sconv_sample_kernel(h_ref, b_ref, c_ref, e_ref, w_ref, v_ref, ch_ref, *, ds):
    ch = c_ref[...].astype(F32) * h_ref[...].astype(F32)
    u = _conv_from_taps(_sample_taps(ch, e_ref[...], 3, ds), w_ref[...])
    v_ref[...] = (b_ref[...].astype(F32) * u).astype(v_ref.dtype)
    ch_ref[...] = ch


def _sconv(cfg, proj, e_sc, w):
    d = cfg.d
    tc = _largest_tile(d, 512)
    nb = d // tc
    pblk = lambda off: pl.BlockSpec((cfg.lp, tc), lambda b, j: (b, j + off * nb))
    blk = 4 * _nbytes((cfg.lp, tc), BF16)
    v_p, st_p = pl.pallas_call(
        functools.partial(_sconv_prompt_kernel, lp=cfg.lp),
        grid=(cfg.bp, nb),
        in_specs=[pblk(2), pblk(3), pblk(4), pl.BlockSpec((3, tc), lambda b, j: (0, j))],
        out_specs=[pl.BlockSpec((cfg.lp, tc), lambda b, j: (b, j)),
                   pl.BlockSpec((1, 2, tc), lambda b, j: (b, 0, j))],
        out_shape=[jax.ShapeDtypeStruct((cfg.bp * cfg.lp, d), BF16),
                   jax.ShapeDtypeStruct((cfg.bp, 2, d), F32)],
        compiler_params=_cparams(("parallel", "parallel"), blk, 6 * _nbytes((cfg.lp, tc), F32)),
        name="sconv_prompt",
    )(proj, proj, proj, w)
    sblk = lambda off: pl.BlockSpec((cfg.tm, tc), lambda j: (cfg.sb, j + off * nb))
    own = pl.BlockSpec((cfg.tm, tc), lambda j: (0, j))
    blk = 4 * _nbytes((cfg.tm, tc), BF16) + 2 * _nbytes((cfg.tm, tc), F32)
    v_s, ch_s = pl.pallas_call(
        functools.partial(_sconv_sample_kernel, ds=cfg.ds),
        grid=(nb,),
        in_specs=[sblk(2), sblk(3), sblk(4), own, pl.BlockSpec((3, tc), lambda j: (0, j))],
        out_specs=[own, own],
        out_shape=[jax.ShapeDtypeStruct((cfg.tm, d), BF16),
                   jax.ShapeDtypeStruct((cfg.tm, d), F32)],
        compiler_params=_cparams(("parallel",), blk, 6 * _nbytes((cfg.tm, tc), F32)),
        name="sconv_sample",
    )(proj, proj, proj, e_sc, w)
    st_s = ch_s[:cfg.db * cfg.ds].reshape(cfg.db, cfg.ds, d)[:, cfg.ds - 2:]
    return v_p, v_s, st_p, st_s


def _tile_taps(x, halo, k):
    row8 = _rows((SUBLANES, x.shape[1]))
    taps = [x]
    for s in range(1, k):
        body = _shift_down(x, s)
        top = jnp.where(row8 >= s, body[:SUBLANES], _shift_down(halo, s))
        taps.append(jnp.concatenate([top, body[SUBLANES:]], axis=0))
    return taps


def _ffn_up_kernel(x_ref, wa_ref, wg_ref, ea_ref, eg_ref, cwa_ref, cwg_ref, cba_ref, cbg_ref, *refs, cfg, n_cast):
    h_ref, pst_ref, ups_ref = refs[n_cast:n_cast + 3]
    raw_ref, halo_ref = refs[-2:]
    _run_casts(refs[:n_cast], refs[n_cast + 3:-2])
    i = pl.program_id(0)
    j = pl.program_id(1)
    tm = cfg.tm

    @pl.when(jnp.logical_and(i == 0, j == 0))
    def _():
        raw_ref[...] = jnp.zeros_like(raw_ref)
        halo_ref[...] = jnp.zeros_like(halo_ref)

    def variant(sample):
        x = x_ref[...]
        raw_ref[0, SUBLANES:, :] = _dot(x, wa_ref[...])
        raw_ref[1, SUBLANES:, :] = _dot(x, wg_ref[...])
        lag = lambda half, s: raw_ref[half, pl.ds(SUBLANES - s, tm), :]
        if sample:
            t = _rows((tm, 1)) % cfg.ds
            taps = [[lag(half, 0)] + [jnp.where(t >= s, lag(half, s), e_ref[pl.ds(2 - s, tm), :]) for s in (1, 2)]
                    for half, e_ref in ((0, ea_ref), (1, eg_ref))]
            ups_ref[0] = taps[0][0]
            ups_ref[1] = taps[1][0]
        else:
            inside = i % cfg.nt != 0
            for half in (0, 1):
                raw_ref[half, :SUBLANES, :] = jnp.where(inside, halo_ref[j, half], 0.0)
            taps = [[lag(half, s) for s in range(3)] for half in (0, 1)]
        act = _conv_from_taps(taps[0], cwa_ref[...], cba_ref[...])
        gate = _conv_from_taps(taps[1], cwg_ref[...], cbg_ref[...])
        h_ref[...] = (_silu(act) * gate).astype(h_ref.dtype)
        for half in (0, 1):
            tail = raw_ref[half, tm:, :]
            pst_ref[0, half] = tail[SUBLANES - 2:]
            halo_ref[j, half] = tail

    pl.when(i < cfg.sb)(functools.partial(variant, False))
    pl.when(i == cfg.sb)(functools.partial(variant, True))


def _ffn_up(cfg, xn, w_up, e_fc, cw, cb, casts=()):
    d_ff, tm, sb, nt = cfg.d_ff, cfg.tm, cfg.sb, cfg.nt
    k = xn.shape[1]
    tn = _largest_tile(d_ff, 512)
    nb = d_ff // tn
    ni = cfg.r // tm
    only_sb = lambda i, j: jnp.where(i == sb, j, 0)
    cast_specs = [c.specs(ni * nb, lambda i, j: i * nb + j) for c in casts]
    blk = (_nbytes((tm, k), BF16) + 2 * _nbytes((k, tn), BF16) + 2 * _nbytes((tm, tn), F32)
           + _nbytes((tm, tn), BF16) + 2 * _nbytes((tm, tn), F32) + sum(s[3] for s in cast_specs))
    h, pst, ups, *cast_out = pl.pallas_call(
        functools.partial(_ffn_up_kernel, cfg=cfg, n_cast=len(casts)),
        grid=(ni, nb),
        in_specs=[pl.BlockSpec((tm, k), lambda i, j: (i, 0)),
                  pl.BlockSpec((k, tn), lambda i, j: (0, j)),
                  pl.BlockSpec((k, tn), lambda i, j: (0, j + nb)),
                  pl.BlockSpec((tm + SUBLANES, tn), lambda i, j: (0, only_sb(i, j))),
                  pl.BlockSpec((tm + SUBLANES, tn), lambda i, j: (0, only_sb(i, j) + nb)),
                  pl.BlockSpec((3, tn), lambda i, j: (0, j)), pl.BlockSpec((3, tn), lambda i, j: (0, j + nb)),
                  pl.BlockSpec((1, tn), lambda i, j: (0, j)), pl.BlockSpec((1, tn), lambda i, j: (0, j + nb))]
                 + [s[0] for s in cast_specs],
        out_specs=[pl.BlockSpec((tm, tn), lambda i, j: (i, j)),
                   pl.BlockSpec((1, 2, 2, tn), lambda i, j: (i, 0, 0, j)),
                   pl.BlockSpec((2, tm, tn), lambda i, j: (0, 0, only_sb(i, j)))] + [s[1] for s in cast_specs],
        out_shape=[jax.ShapeDtypeStruct((cfg.r, d_ff), BF16),
                   jax.ShapeDtypeStruct((ni, 2, 2, d_ff), F32),
                   jax.ShapeDtypeStruct((2, tm, d_ff), F32)] + [s[2] for s in cast_specs],
        scratch_shapes=[pltpu.VMEM((2, SUBLANES + tm, tn), F32), pltpu.VMEM((nb, 2, SUBLANES, tn), F32)],
        compiler_params=_cparams(("arbitrary", "arbitrary"), blk, 16 * _nbytes((tm, tn), F32)),
        name="ffn_up",
    )(xn, w_up, w_up, e_fc, e_fc, cw, cw, cb, cb, *[c.src for c in casts])
    last = pst[nt - 1:sb:nt]
    p_fc = jnp.concatenate([last[:, 0], last[:, 1]], axis=-1)
    return h, p_fc, ups, cast_out


def _ssd_gate_norm(y, z, gnw):
    y = y * _silu(z)
    return y * lax.rsqrt(jnp.mean(y * y, axis=-1, keepdims=True) + EPS) * gnw


def _ssd_prompt_kernel(x_ref, bm_ref, cm_ref, z_ref, dtr_ref, wx_ref, wb_ref, wc_ref, bx_ref, bb_ref, bc_ref,
                       dtb_ref, alog_ref, dsk_ref, gnw_ref, e_ref, tril_ref,
                       y_ref, hfin_ref, mcx_ref, mcb_ref, mcc_ref,
                       ht_ref, hx_ref, hb_ref, hc_ref, *, cfg):
    lp, hpg, n, gw = cfg.lp, cfg.hpg, cfg.n, cfg.gw
    c = SSD_CHUNK
    nch = 1 + cfg.seq // c
    ht_ref[...] = jnp.zeros_like(ht_ref)
    hx_ref[...] = jnp.zeros_like(hx_ref)
    hb_ref[...] = jnp.zeros_like(hb_ref)
    hc_ref[...] = jnp.zeros_like(hc_ref)
    a_neg = -jnp.exp(alog_ref[0])
    dtb = dtb_ref[0]
    e = e_ref[...]
    tril = tril_ref[...]
    li = lax.broadcasted_iota(jnp.int32, (c, c), 0)
    si = lax.broadcasted_iota(jnp.int32, (c, c), 1)
    causal = li >= si
    lane_lo = lax.broadcasted_iota(jnp.int32, (c, 2 * HEAD_DIM), 1) < HEAD_DIM

    def conv_silu(src_ref, nxt, buf_ref, w_ref, b_ref, first):
        y = _silu(_conv_from_taps([buf_ref[pl.ds(SUBLANES - s, c), :] for s in range(4)], w_ref[...], b_ref[...]))
        buf_ref[:SUBLANES, :] = jnp.where(first, buf_ref[N_META:N_META + SUBLANES, :], buf_ref[c:, :])
        buf_ref[SUBLANES:, :] = src_ref[nxt, :].astype(F32)
        return y

    for src_ref, buf_ref in ((x_ref, hx_ref), (bm_ref, hb_ref), (cm_ref, hc_ref)):
        buf_ref[SUBLANES:, :] = src_ref[:c, :].astype(F32)

    def chunk(k, carry):
        first = k == 0
        start = pl.multiple_of(jnp.where(first, 0, N_META + c * (k - 1)), BF16_ROWS)
        rows = pl.ds(start, c)
        nxt = pl.ds(pl.multiple_of(jnp.minimum(N_META + c * k, lp - c), BF16_ROWS), c)
        xh = conv_silu(x_ref, nxt, hx_ref, wx_ref, bx_ref, first)
        bc = conv_silu(bm_ref, nxt, hb_ref, wb_ref, bb_ref, first)
        cc = conv_silu(cm_ref, nxt, hc_ref, wc_ref, bc_ref, first)
        live = jnp.logical_or(jnp.logical_not(first), _rows((c, LANES)) < N_META)
        dt = jnp.where(live, _softplus(_group_lanes(dtr_ref[rows, :], pl.program_id(1), hpg) + dtb), 0.0)
        a = dt * a_neg
        a1 = a.astype(BF16)
        r1 = a - a1.astype(F32)
        a2 = r1.astype(BF16)
        a3 = (r1 - a2.astype(F32)).astype(BF16)
        cs = _dot(tril, a1) + _dot(tril, a2) + _dot(tril, a3)
        cs_t = cs.T
        cs_last = cs[c - 1:c]
        ecs = jnp.exp(cs)
        both_e = _dot(jnp.concatenate([dt, dt * jnp.exp(cs_last - cs)], axis=0).astype(BF16), e)
        xs_b = (xh * both_e[:c]).astype(BF16)
        xsw_b = (xh * both_e[c:]).astype(BF16)
        dec_row = _expand_heads(jnp.broadcast_to(ecs[c - 1:c], (SUBLANES, LANES)), e)[:1]
        scores = jnp.where(causal, _dot_nt(cc.astype(BF16), bc.astype(BF16)), 0.0)
        ht = ht_ref[...]
        ht_b = ht.astype(BF16)
        pairs = []
        for q in range(hpg // 2):
            cols = slice(2 * HEAD_DIM * q, 2 * HEAD_DIM * (q + 1))
            rhs = jnp.concatenate([xs_b[:, cols], ht_b[:, cols]], axis=0)
            res = []
            for r in (2 * q, 2 * q + 1):
                m = scores * jnp.exp(jnp.minimum(cs[:, r:r + 1] - cs_t[r:r + 1, :], 0.0))
                lhs = jnp.concatenate([m, cc * ecs[:, r:r + 1]], axis=1).astype(BF16)
                res.append(_dot(lhs, rhs))
            pairs.append(jnp.where(lane_lo, res[0], res[1]))
        y = jnp.concatenate(pairs, axis=1) if len(pairs) > 1 else pairs[0]
        ht_ref[...] = ht * dec_row + _dot(bc.T.astype(BF16), xsw_b)
        y = y + dsk_ref[...] * xh
        y_ref[rows, :] = _ssd_gate_norm(y, z_ref[rows, :].astype(F32), gnw_ref[...]).astype(y_ref.dtype)
        return carry

    lax.fori_loop(0, nch, chunk, 0)
    hfin_ref[0] = ht_ref[...].T.reshape(hpg, HEAD_DIM, n)
    tail = lambda ref: ref[lp - BF16_ROWS:lp, :].astype(F32)[BF16_ROWS - 3:]
    mcx_ref[0] = tail(x_ref)
    mcb_ref[0] = tail(bm_ref)
    mcc_ref[0] = tail(cm_ref)


def _ssd_prompt(cfg, proj, dtr_g, prm):
    lp, gw, n, hpg, g_ = cfg.lp, cfg.gw, cfg.n, cfg.hpg, N_GROUPS
    xb0 = 7 * cfg.d // gw
    zb0 = 5 * cfg.d // gw
    bb0 = 9 * cfg.d // n
    cvb = cfg.d_inner // n
    colblk = lambda w, f: pl.BlockSpec((lp, w), f)
    par = lambda rows, w, f: pl.BlockSpec((rows, w), f)
    vec3 = pl.BlockSpec((1, 1, LANES), lambda b, g: (g, 0, 0))
    in_specs = [
        colblk(gw, lambda b, g: (b, xb0 + g)), colblk(n, lambda b, g: (b, bb0 + g)),
        colblk(n, lambda b, g: (b, bb0 + g_ + g)), colblk(gw, lambda b, g: (b, zb0 + g)),
        colblk(LANES, lambda b, g: (b, 0)),
        par(4, gw, lambda b, g: (0, g)), par(4, n, lambda b, g: (0, cvb + g)), par(4, n, lambda b, g: (0, cvb + g_ + g)),
        par(1, gw, lambda b, g: (0, g)), par(1, n, lambda b, g: (0, cvb + g)), par(1, n, lambda b, g: (0, cvb + g_ + g)),
        vec3, vec3,
        par(1, gw, lambda b, g: (0, g)), par(1, gw, lambda b, g: (0, g)),
        par(LANES, gw, lambda b, g: (0, 0)), par(SSD_CHUNK, SSD_CHUNK, lambda b, g: (0, 0)),
    ]
    out_specs = [
        pl.BlockSpec((lp, gw), lambda b, g: (b, g)),
        pl.BlockSpec((1, hpg, HEAD_DIM, n), lambda b, g: (b, g, 0, 0)),
        pl.BlockSpec((1, 3, gw), lambda b, g: (b, 0, g)),
        pl.BlockSpec((1, 3, n), lambda b, g: (b, 0, g)),
        pl.BlockSpec((1, 3, n), lambda b, g: (b, 0, g)),
    ]
    out_shape = [
        jax.ShapeDtypeStruct((cfg.bp * cfg.lp, cfg.d_inner), BF16),
        jax.ShapeDtypeStruct((cfg.bp, cfg.h, HEAD_DIM, n), F32),
        jax.ShapeDtypeStruct((cfg.bp, 3, cfg.d_inner), F32),
        jax.ShapeDtypeStruct((cfg.bp, 3, g_ * n), F32),
        jax.ShapeDtypeStruct((cfg.bp, 3, g_ * n), F32),
    ]
    blk = 3 * _nbytes((lp, gw), BF16) + 2 * _nbytes((lp, n), BF16) + _nbytes((lp, LANES), F32)
    y, hfin, mcx, mcb, mcc = pl.pallas_call(
        functools.partial(_ssd_prompt_kernel, cfg=cfg),
        grid=(cfg.bp, g_),
        in_specs=in_specs, out_specs=out_specs, out_shape=out_shape,
        scratch_shapes=[pltpu.VMEM((n, gw), F32), pltpu.VMEM((SUBLANES + SSD_CHUNK, gw), F32),
                        pltpu.VMEM((SUBLANES + SSD_CHUNK, n), F32), pltpu.VMEM((SUBLANES + SSD_CHUNK, n), F32)],
        compiler_params=_cparams(("parallel", "parallel"), blk, 24 * _nbytes((SSD_CHUNK, gw), F32) + (4 << 20)),
        name="ssd_prompt",
    )(proj, proj, proj, proj, dtr_g,
      prm["mconv_w"], prm["mconv_w"], prm["mconv_w"], prm["mconv_b"], prm["mconv_b"], prm["mconv_b"],
      prm["dt_bias_g"], prm["a_log_g"], prm["d_skip_e"], prm["gnorm_w"], prm["expand"], prm["tril"])
    return y, hfin, jnp.concatenate([mcx, mcb, mcc], axis=-1)


def _ssd_tok_kernel(x_ref, bm_ref, cm_ref, dtr_ref, ex_ref, eb_ref, ec_ref, wx_ref, wb_ref, wc_ref,
                    bx_ref, bb_ref, bc_ref, dtb_ref, alog_ref, dsk_ref, e_ref,
                    ypre_ref, ecs_ref, xsw_ref, bo_ref, co_ref, dec_ref, *, ds, hpg):
    e = e_ref[...]
    conv = lambda ref, eref, w, b: _silu(_conv_from_taps(
        _sample_taps(ref[...].astype(F32), eref[...], 4, ds), w[...], b[...]))
    xh = conv(x_ref, ex_ref, wx_ref, bx_ref)
    bc = conv(bm_ref, eb_ref, wb_ref, bb_ref)
    cc = conv(cm_ref, ec_ref, wc_ref, bc_ref)
    dt = _softplus(_group_lanes(dtr_ref[...], pl.program_id(0), hpg) + dtb_ref[0])
    a = dt * (-jnp.exp(alog_ref[0]))
    t = _rows(a.shape) % ds
    t1 = _rows((a.shape[0], 1)) % ds
    cs = a
    for j in range(1, ds):
        cs = cs + jnp.where(t >= j, _shift_down(a, j), 0.0)
    cs_last = jnp.zeros_like(cs)
    for j in range(ds):
        cs_last = cs_last + jnp.where(t == ds - 1 - j, _shift_up(cs, j), 0.0)
    ecs_e = _expand_heads(jnp.exp(cs), e)
    xs = xh * _expand_heads(dt, e)
    y = dsk_ref[...] * xh
    for j in range(ds):
        sc = jnp.sum(cc * _shift_down(bc, j), axis=-1, keepdims=True)
        diff = jnp.where(t >= j, cs - _shift_down(cs, j), 0.0)
        coef = jnp.where(t1 >= j, sc, 0.0) * jnp.exp(diff)
        y = y + _expand_heads(coef, e) * _shift_down(xs, j)
    ypre_ref[...] = y
    ecs_ref[...] = ecs_e
    xsw_ref[...] = xs * _expand_heads(jnp.exp(cs_last - cs), e)
    bo_ref[...] = bc
    co_ref[...] = cc
    dec_ref[...] = jnp.exp(cs_last)


def _ssd_state_kernel(dec_ref, h0_ref, c_ref, b_ref, xsw_ref, *refs, cfg):
    hout_ref, yoff_ref = refs[-2:]
    hpg, n, gw, ds = cfg.hpg, cfg.n, cfg.gw, cfg.ds
    i = pl.program_id(0)
    nseq = SUBLANES // ds
    pad = jnp.zeros((LANES - SUBLANES, n), F32)
    row8 = _rows((SUBLANES, 1)) // ds
    cmat = c_ref[...]
    bmat = b_ref[...]
    xsw = xsw_ref[...]
    youts = []
    for g in range(N_GROUPS):
        cg = cmat[:, g * n:(g + 1) * n].astype(BF16)
        bg = bmat[:, g * n:(g + 1) * n]
        xw = jnp.concatenate([xsw[:, g * gw:(g + 1) * gw], jnp.zeros((LANES - SUBLANES, gw), F32)], axis=0)
        xw_t = xw.T.astype(BF16)
        yg = jnp.zeros((SUBLANES, gw), F32)
        for s in range(nseq):
            h0 = h0_ref[s, g * hpg:(g + 1) * hpg].reshape(gw, n)
            yg = jnp.where(row8 == s, _dot_nt(cg, h0.astype(BF16)), yg)
            bs = jnp.concatenate([jnp.where(row8 == s, bg, 0.0), pad], axis=0).astype(BF16)
            dh = _dot(xw_t, bs)
            for r in range(hpg):
                sl = slice(r * HEAD_DIM, (r + 1) * HEAD_DIM)
                hout_ref[s, g * hpg + r] = h0[sl] * dec_ref[i * nseq + s, g * hpg + r] + dh[sl]
        youts.append(yg)
    yoff_ref[...] = jnp.concatenate(youts, axis=1)


def _ssd_post_kernel(ypre_ref, yoff_ref, ecs_ref, z_ref, gnw_ref, o_ref):
    y = ypre_ref[...] + yoff_ref[...] * ecs_ref[...]
    o_ref[...] = _ssd_gate_norm(y, z_ref[...].astype(F32), gnw_ref[...]).astype(o_ref.dtype)


def _ssd_sample(cfg, proj, dtr_g, e_mc, h_all, layer, h_new, prm):
    tm, gw, n, g_, sb = cfg.tm, cfg.gw, cfg.n, N_GROUPS, cfg.sb
    xb0 = 7 * cfg.d // gw
    zb0 = 5 * cfg.d // gw
    bb0 = 9 * cfg.d // n
    cvb = cfg.d_inner // n
    row = lambda w, f: pl.BlockSpec((tm, w), f)
    par = lambda rows, w, f: pl.BlockSpec((rows, w), f)
    vec3 = pl.BlockSpec((1, 1, LANES), lambda g: (g, 0, 0))
    in_specs = [
        row(gw, lambda g: (sb, xb0 + g)), row(n, lambda g: (sb, bb0 + g)), row(n, lambda g: (sb, bb0 + g_ + g)),
        row(LANES, lambda g: (sb, 0)),
        row(gw, lambda g: (0, g)), row(n, lambda g: (0, cvb + g)), row(n, lambda g: (0, cvb + g_ + g)),
        par(4, gw, lambda g: (0, g)), par(4, n, lambda g: (0, cvb + g)), par(4, n, lambda g: (0, cvb + g_ + g)),
        par(1, gw, lambda g: (0, g)), par(1, n, lambda g: (0, cvb + g)), par(1, n, lambda g: (0, cvb + g_ + g)),
        vec3, vec3, par(1, gw, lambda g: (0, g)), par(LANES, gw, lambda g: (0, 0)),
    ]
    own = lambda w: pl.BlockSpec((tm, w), lambda g: (0, g))
    blk = 2 * _nbytes((tm, gw), BF16) + 5 * _nbytes((tm, gw), F32)
    ypre, ecs, xsw, bo, co, dec = pl.pallas_call(
        functools.partial(_ssd_tok_kernel, ds=cfg.ds, hpg=cfg.hpg),
        grid=(g_,),
        in_specs=in_specs,
        out_specs=[own(gw), own(gw), own(gw), own(n), own(n), own(LANES)],
        out_shape=[jax.ShapeDtypeStruct((tm, cfg.d_inner), F32)] * 3
                  + [jax.ShapeDtypeStruct((tm, g_ * n), F32)] * 2
                  + [jax.ShapeDtypeStruct((tm, g_ * LANES), F32)],
        compiler_params=_cparams(("parallel",), blk, 16 * _nbytes((tm, gw), F32)),
        name="ssd_tok",
    )(proj, proj, proj, dtr_g, e_mc, e_mc, e_mc,
      prm["mconv_w"], prm["mconv_w"], prm["mconv_w"], prm["mconv_b"], prm["mconv_b"], prm["mconv_b"],
      prm["dt_bias_g"], prm["a_log_g"], prm["d_skip_e"], prm["expand"])
    nrow = cfg.db * cfg.ds
    dec_sh = dec[:nrow].reshape(cfg.db, cfg.ds, g_, LANES)[:, cfg.ds - 1, :, :cfg.hpg].reshape(cfg.db, cfg.h)
    nseq = SUBLANES // cfg.ds
    rows8 = lambda w: pl.BlockSpec((SUBLANES, w), lambda i: (i, 0))
    hblk = pl.BlockSpec((None, nseq, cfg.h, HEAD_DIM, n), lambda i: (layer, i, 0, 0, 0))
    blk = 2 * _nbytes((nseq, cfg.h, HEAD_DIM, n), F32) + 3 * _nbytes((SUBLANES, cfg.d_inner), F32)
    hbm = pl.BlockSpec(memory_space=pl.ANY)
    aliased = [jnp.zeros((tm, cfg.d_inner), F32)] + ([] if h_new is None else [h_new])
    hout, yoff = pl.pallas_call(
        functools.partial(_ssd_state_kernel, cfg=cfg),
        grid=(cfg.db // nseq,),
        in_specs=[pl.BlockSpec(memory_space=pltpu.SMEM), hblk, rows8(g_ * n), rows8(g_ * n), rows8(cfg.d_inner)]
                 + [hbm] * len(aliased),
        out_specs=[hblk, rows8(cfg.d_inner)],
        out_shape=[jax.ShapeDtypeStruct(h_all.shape, F32), jax.ShapeDtypeStruct((tm, cfg.d_inner), F32)],
        input_output_aliases={5: 1} if h_new is None else {5: 1, 6: 0},
        compiler_params=_cparams(("arbitrary",), blk, 8 << 20),
        name="ssd_state",
    )(dec_sh, h_all, co, bo, xsw, *aliased)
    blk = 3 * _nbytes((tm, gw), F32) + 2 * _nbytes((tm, gw), BF16)
    yg = pl.pallas_call(
        _ssd_post_kernel,
        grid=(g_,),
        in_specs=[own(gw), own(gw), own(gw), row(gw, lambda g: (sb, zb0 + g)), par(1, gw, lambda g: (0, g))],
        out_specs=own(gw),
        out_shape=jax.ShapeDtypeStruct((tm, cfg.d_inner), BF16),
        compiler_params=_cparams(("parallel",), blk, 4 * _nbytes((tm, gw), F32)),
        name="ssd_post",
    )(ypre, yoff, ecs, proj, prm["gnorm_w"])
    return yg, hout


def _flat_state(cfg, buf):
    db, km1, c = buf.shape
    nseq = (cfg.tm + SUBLANES) // cfg.ds
    assert nseq * cfg.ds == cfg.tm + SUBLANES
    return jnp.pad(buf.astype(F32), ((0, nseq - db), (0, cfg.ds - km1), (0, 0))).reshape(nseq * cfg.ds, c)


def _dt_weights(cfg, w_in):
    assert cfg.h <= LANES
    w_dt = lax.optimization_barrier(w_in[:, :, cfg.nmain:])
    return jnp.pad(w_dt, ((0, 0), (0, 0), (0, LANES - cfg.h))).astype(BF16)


def _layer_params(cfg, sconv_w, mconv_w, mconv_b, dt_bias, a_log, d_skip, gnorm_w, fconv_w, fconv_b):
    g_, hpg = N_GROUPS, cfg.hpg
    grp = lambda v: jnp.pad(v.astype(F32).reshape(g_, 1, hpg), ((0, 0), (0, 0), (0, LANES - hpg)))
    head = jnp.arange(LANES)[:, None]
    col = jnp.arange(cfg.gw)[None, :] // HEAD_DIM
    return dict(
        sconv_w=sconv_w.astype(F32),
        mconv_w=mconv_w.astype(F32), mconv_b=mconv_b.astype(F32).reshape(1, -1),
        dt_bias_g=grp(dt_bias), a_log_g=grp(a_log),
        d_skip_e=jnp.repeat(d_skip.astype(F32), HEAD_DIM).reshape(1, -1),
        gnorm_w=gnorm_w.astype(F32).reshape(1, -1),
        fconv_w=fconv_w.astype(F32), fconv_b=fconv_b.astype(F32).reshape(1, -1),
        expand=((head == col) & (head < hpg)).astype(BF16),
        tril=(jnp.arange(SSD_CHUNK)[:, None] >= jnp.arange(SSD_CHUNK)[None, :]).astype(BF16),
    )


def _layer(cfg, x, xn, w_in_b, w_dt_b, f32_mats, layer, prm, st_sc, st_mc, h_all, h_new, st_fc):
    d, d_ff = cfg.d, cfg.d_ff
    nrow = cfg.db * cfg.ds
    srows = slice(cfg.sb * cfg.tm, cfg.sb * cfg.tm + nrow)
    own = ("w_up", "w_down", "wa_out", "wb_out", "w_o")
    proj, *cast = _mm(cfg, xn, w_in_b, cfg.nmain, BF16, "proj", casts=[Cast(f32_mats[n], layer) for n in own])
    mats = dict(zip(own, cast))
    dtr_g = _mm(cfg, xn, w_dt_b, LANES, F32, "proj_dt")
    v, v_s, p_sc, s_sc = _sconv(cfg, proj, _flat_state(cfg, st_sc), prm["sconv_w"])
    out_a = _mm(cfg, v, mats["wa_out"], d, BF16, "wa_out", x_tail=v_s)
    yg, p_h, p_mc = _ssd_prompt(cfg, proj, dtr_g, prm)
    yg_s, h_new = _ssd_sample(cfg, proj, dtr_g, _flat_state(cfg, st_mc), h_all, layer, h_new, prm)
    s_mc = proj[srows, 7 * d:7 * d + cfg.conv_dim].astype(F32).reshape(cfg.db, cfg.ds, cfg.conv_dim)[:, cfg.ds - 3:]
    mix = _mm(cfg, yg, mats["wb_out"], d, BF16, "wb_out", epilogue="mix", x_tail=yg_s,
              extras=((proj, 0, BF16), (proj, 1, BF16), (out_a, 0, BF16)))
    x = _mm(cfg, mix, mats["w_o"], d, F32, "w_o", epilogue="resid", extras=((x, 0, F32),))
    xn2 = _rmsnorm(cfg, x, prm["norm2_g"], BF16)
    more = layer + 1 < f32_mats["w_in"].shape[0]
    hmid, p_fc, ups, nxt = _ffn_up(cfg, xn2, mats["w_up"], _flat_state(cfg, st_fc), prm["fconv_w"], prm["fconv_b"],
                                   casts=[Cast(f32_mats["w_in"], layer + 1)] if more else [])
    s_fc = jnp.concatenate([ups[0, :nrow], ups[1, :nrow]], axis=-1).reshape(cfg.db, cfg.ds, 2 * d_ff)[:, cfg.ds - 2:]
    x = _mm(cfg, hmid, mats["w_down"], d, F32, "w_down", epilogue="resid", extras=((x, 0, F32),))
    return x, h_new, (p_sc, p_mc, p_h, p_fc, s_sc, s_mc, s_fc), (nxt[0] if more else None)


def kernel(x_prompt, x_sample, state_sconv, state_mconv, state_ssm, state_fconv, meta_tokens, norm1_g, w_in,
           sconv_w, wa_out, mconv_w, mconv_b, dt_bias, a_log, d_skip, gnorm_w, wb_out, w_o, norm2_g, w_up,
           fconv_w, fconv_b, w_down, final_g):
    cfg = _make_cfg(x_prompt, x_sample, state_ssm, w_up)
    depth = w_in.shape[0]
    dtype = x_prompt.dtype
    nrow = cfg.db * cfg.ds
    meta = meta_tokens.astype(dtype)
    x = jnp.concatenate([piece for b in range(cfg.bp) for piece in (meta, x_prompt[b])]
                        + [x_sample.reshape(nrow, cfg.d),
                           jnp.zeros((cfg.r - cfg.bp * cfg.lp - nrow, cfg.d), dtype)], axis=0).astype(F32)
    f32_mats = dict(w_in=w_in, wa_out=wa_out, wb_out=wb_out, w_o=w_o, w_up=w_up, w_down=w_down)
    w_dt_b = _dt_weights(cfg, w_in)
    w_in_b = _cast_layer(w_in, 0)
    states = []
    h_new = None
    for l in range(depth):
        prm = _layer_params(cfg, sconv_w[l], mconv_w[l], mconv_b[l], dt_bias[l], a_log[l], d_skip[l], gnorm_w[l],
                            fconv_w[l], fconv_b[l])
        prm["norm2_g"] = norm2_g[l]
        xn = _rmsnorm(cfg, x, norm1_g[l], BF16)
        x, h_new, st, w_in_b = _layer(cfg, x, xn, w_in_b, w_dt_b[l], f32_mats, l, prm, state_sconv[l],
                                      state_mconv[l], state_ssm, h_new, state_fconv[l])
        states.append(st)
    y_prompt = _rmsnorm_prompt_tokens(cfg, x, final_g).astype(dtype)
    y_s = _rmsnorm(cfg, x, final_g, F32, cfg.sb * cfg.tm, cfg.tm)
    y_sample = y_s[:nrow].reshape(cfg.db, cfg.ds, cfg.d).astype(dtype)
    p_sc, p_mc, p_h, p_fc, s_sc, s_mc, s_fc = [jnp.stack([s[i] for s in states]).astype(dtype) for i in range(7)]
    return (y_prompt, y_sample, p_sc, p_mc, p_h, p_fc, s_sc, s_mc, h_new.astype(dtype), s_fc)
```
